```python
import jax, jax.numpy as jnp
from jax import lax
import numpy as np

D_MODEL = 2048
BATCH = 2
SEQ = 4096
DEPTH = 2
DEC_BATCH = 8
DEC_SEQ = 4
PAST_LEN = 16384
PAGE_SIZE = 128

MIX_WIDTH = D_MODEL
RET_HEADS = 4
RET_DIM = D_MODEL // 16
RET_WIDTH = RET_HEADS * RET_DIM
RET_CHUNK = 128
ROPE_BASE = 10000.0
CONV_WIDTH = D_MODEL // 4
CONV_K = 31
SB_HEADS = 8
SB_DIM = D_MODEL // 16
SB_WIDTH = SB_HEADS * SB_DIM
SB_BLOCK = 128
SB_BIAS_INIT = -6.0
IN_WIDTH = 4 * RET_WIDTH + 2 * CONV_WIDTH + 3 * SB_WIDTH
N_GROUPS = 4
EXPERTS_PER_GROUP = 8
N_EXPERTS = N_GROUPS * EXPERTS_PER_GROUP
TOP_K = 2
D_EXPERT = D_MODEL // 4
MOE_BLOCK = 128
EPS = 1e-6

kernel_name = 'hybrid_retention_conv_stickbreak_hmoe_step'


def rms_norm(x, g):
    xf = x.astype(jnp.float32)
    y = xf * lax.rsqrt(jnp.mean(xf * xf, axis=-1, keepdims=True) + EPS)
    return (y * g.astype(jnp.float32)).astype(x.dtype)


def layer_norm(x, g, b):
    xf = x.astype(jnp.float32)
    mu = jnp.mean(xf, axis=-1, keepdims=True)
    var = jnp.mean(jnp.square(xf - mu), axis=-1, keepdims=True)
    return (xf - mu) * lax.rsqrt(var + EPS) * g.astype(jnp.float32) + b.astype(jnp.float32)


def rotary(x, pos):
    half = x.shape[-1] // 2
    inv = ROPE_BASE ** (-jnp.arange(half, dtype=jnp.float32) / half)
    ang = pos.astype(jnp.float32)[:, None] * inv[None, :]
    cos = jnp.cos(ang)[None, :, None, :]
    sin = jnp.sin(ang)[None, :, None, :]
    x1, x2 = x[..., :half], x[..., half:]
    return jnp.concatenate([x1 * cos - x2 * sin, x1 * sin + x2 * cos], axis=-1)


def retention(q, k, v, s0):
    B, T, H, _ = q.shape
    C = min(RET_CHUNK, T)
    n = T // C
    log_gamma = jnp.log(1.0 - 2.0 ** (-5.0 - jnp.arange(H, dtype=jnp.float32)))
    i = jnp.arange(C, dtype=jnp.float32)
    diff = i[:, None] - i[None, :]
    dmat = jnp.where(diff >= 0, jnp.exp(log_gamma[:, None, None] * jnp.maximum(diff, 0.0)), 0.0)
    q_decay = jnp.exp(log_gamma[None, :] * (i[:, None] + 1.0))
    k_decay = jnp.exp(log_gamma[:, None] * (C - 1.0 - i[None, :]))
    c_decay = jnp.exp(log_gamma * C)

    def to_chunks(t):
        return t.astype(jnp.float32).reshape(B, n, C, H, t.shape[-1]).swapaxes(0, 1)

    def step(S, inp):
        qc, kc, vc = inp
        att = jnp.einsum('bihd,bjhd->bhij', qc, kc) * dmat
        o = jnp.einsum('bhij,bjhe->bihe', att, vc)
        o = o + jnp.einsum('bihd,bhde->bihe', qc, S) * q_decay[:, :, None]
        S = S * c_decay[:, None, None] + jnp.einsum('bjhd,hj,bjhe->bhde', kc, k_decay, vc)
        return S, o

    S, o = lax.scan(step, s0.astype(jnp.float32), (to_chunks(q), to_chunks(k), to_chunks(v)))
    return o.swapaxes(0, 1).reshape(B, T, H, v.shape[-1]), S


def causal_depthwise_conv(u, buf, w, b):
    ub = jnp.concatenate([buf.astype(u.dtype), u], axis=1)
    y = lax.conv_general_dilated(ub, w[:, None, :].astype(u.dtype), window_strides=(1,), padding='VALID',
                                 dimension_numbers=('NWC', 'WIO', 'NWC'), feature_group_count=u.shape[-1])
    return y + b.astype(y.dtype), ub[:, ub.shape[1] - (CONV_K - 1):]


def stick_breaking(q, k, v, bias, q_start):
    B, Tq, H, Dh = q.shape
    Tk = k.shape[1]
    blk = min(SB_BLOCK, Tq)
    nb = Tq // blk
    qb = q.reshape(B, nb, blk, H, Dh).swapaxes(0, 1)
    k_pos = jnp.arange(Tk)
    scale = Dh ** -0.5
    bias_f = bias.astype(jnp.float32)[None, :, None, None]

    def block(args):
        qblk, bi = args
        q_pos = q_start + bi * blk + jnp.arange(blk)
        z = jnp.einsum('bqhd,bkhd->bhqk', qblk, k).astype(jnp.float32) * scale + bias_f
        causal = k_pos[None, :] < q_pos[:, None]
        log_1m = jnp.where(causal, jax.nn.log_sigmoid(-z), 0.0)
        log_a = jax.nn.log_sigmoid(z) + lax.cumsum(log_1m, axis=3, reverse=True) - log_1m
        a = jnp.where(causal, jnp.exp(log_a), 0.0)
        return jnp.einsum('bhqk,bkhd->bqhd', a.astype(v.dtype), v)

    out = lax.map(block, (qb, jnp.arange(nb)))
    return out.swapaxes(0, 1).reshape(B, Tq, H, Dh)


def swiglu(x, wg, wu, wd):
    return (jax.nn.silu(x @ wg) * (x @ wu)) @ wd


def hier_moe(h, router_group_w, router_group_b, router_expert_w, router_expert_b, w_gate, w_up, w_down):
    shape = h.shape
    x = h.reshape(-1, shape[-1])
    N, D = x.shape
    logit_g = (x @ router_group_w).astype(jnp.float32) + router_group_b.astype(jnp.float32)
    p_group = jax.nn.softmax(logit_g, axis=-1)
    g = jnp.argmax(logit_g, axis=-1)
    p_g = jnp.take_along_axis(p_group, g[:, None], axis=1)[:, 0]
    logit_e = ((x @ router_expert_w).astype(jnp.float32) + router_expert_b.astype(jnp.float32)).reshape(N, N_GROUPS, EXPERTS_PER_GROUP)
    logit_e = jnp.take_along_axis(logit_e, g[:, None, None], axis=1)[:, 0]
    top_v, top_i = lax.top_k(logit_e, TOP_K)
    gates = p_g[:, None] * jax.nn.softmax(top_v, axis=-1)
    experts = (g[:, None] * EXPERTS_PER_GROUP + top_i).astype(jnp.int32)

    A = N * TOP_K
    blk = max(8, min(MOE_BLOCK, A // N_EXPERTS))
    n_blocks = (A + N_EXPERTS * (blk - 1) + blk - 1) // blk
    flat_e = experts.reshape(-1)
    flat_tok = jnp.arange(A, dtype=jnp.int32) // TOP_K
    order = jnp.argsort(flat_e)
    sorted_e = flat_e[order]
    counts = jnp.zeros((N_EXPERTS,), jnp.int32).at[flat_e].add(1)
    starts = jnp.cumsum(counts) - counts
    padded = (counts + blk - 1) // blk * blk
    pad_ends = jnp.cumsum(padded)
    pad_starts = pad_ends - padded
    dest = pad_starts[sorted_e] + jnp.arange(A, dtype=jnp.int32) - starts[sorted_e]
    row_tok = jnp.full((n_blocks * blk,), N, jnp.int32).at[dest].set(flat_tok[order])
    blk_expert = jnp.minimum(jnp.searchsorted(pad_ends, jnp.arange(n_blocks, dtype=jnp.int32) * blk, side='right'), N_EXPERTS - 1)
    x_pad = jnp.concatenate([x, jnp.zeros((1, D), x.dtype)], axis=0)
    xb = x_pad[row_tok].reshape(n_blocks, blk, D)

    def run_block(args):
        xblk, e = args
        return swiglu(xblk, w_gate[e], w_up[e], w_down[e])

    yb = lax.map(run_block, (xb, blk_expert)).reshape(n_blocks * blk, D)
    dest_of = jnp.zeros((A,), jnp.int32).at[order].set(dest)
    y = yb[dest_of].reshape(N, TOP_K, D)
    return jnp.einsum('nkd,nk->nd', y, gates.astype(y.dtype)).reshape(shape)


def trunk_layer(x, q_start, ret_s0, conv_buf, k_past, v_past,
                norm_mix, w_in, ret_gn_gain, ret_gn_bias, conv_w, conv_b, conv_ln_gain, conv_ln_bias, sb_bias, w_out,
                norm_ffn, router_group_w, router_group_b, router_expert_w, router_expert_b, w_gate, w_up, w_down):
    B, T, _ = x.shape
    h = rms_norm(x, norm_mix)
    z = h @ w_in
    widths = [RET_WIDTH] * 4 + [CONV_WIDTH] * 2 + [SB_WIDTH] * 3
    cuts = [int(c) for c in np.cumsum(widths)[:-1]]
    rq, rk, rv, rg, ca, cg, sq, sk, sv = jnp.split(z, cuts, axis=-1)
    pos = q_start + jnp.arange(T)

    q_r = rotary(rq.reshape(B, T, RET_HEADS, RET_DIM).astype(jnp.float32), pos)
    k_r = rotary(rk.reshape(B, T, RET_HEADS, RET_DIM).astype(jnp.float32), pos) * RET_DIM ** -0.5
    o_r, s_ret = retention(q_r, k_r, rv.reshape(B, T, RET_HEADS, RET_DIM), ret_s0)
    o_r = layer_norm(o_r, ret_gn_gain.reshape(RET_HEADS, RET_DIM), ret_gn_bias.reshape(RET_HEADS, RET_DIM)).reshape(B, T, RET_WIDTH)
    o_r = (jax.nn.silu(rg.astype(jnp.float32)) * o_r).astype(x.dtype)

    u = ca * jax.nn.sigmoid(cg)
    c, new_buf = causal_depthwise_conv(u, conv_buf, conv_w, conv_b)
    o_c = jax.nn.silu(layer_norm(c, conv_ln_gain, conv_ln_bias)).astype(x.dtype)

    k_new = sk.reshape(B, T, SB_HEADS, SB_DIM)
    v_new = sv.reshape(B, T, SB_HEADS, SB_DIM)
    k_all = jnp.concatenate([k_past.astype(x.dtype), k_new], axis=1)
    v_all = jnp.concatenate([v_past.astype(x.dtype), v_new], axis=1)
    o_s = stick_breaking(sq.reshape(B, T, SB_HEADS, SB_DIM), k_all, v_all, sb_bias, q_start).reshape(B, T, SB_WIDTH)

    x = x + jnp.concatenate([o_r, o_c, o_s.astype(x.dtype)], axis=-1) @ w_out
    x = x + hier_moe(rms_norm(x, norm_ffn), router_group_w, router_group_b, router_expert_w, router_expert_b,
                     w_gate, w_up, w_down)
    return x, s_ret.astype(ret_s0.dtype), new_buf, k_new, v_new


def setup_inputs(seed: int = 0) -> dict:
    key = jax.random.key(seed)
    ks = jax.random.split(key, 32)
    f32 = jnp.float32
    n_pages = PAST_LEN // PAGE_SIZE
    n_pool = (5 * DEC_BATCH * n_pages + 3) // 4

    def nrm(k, shape, scale):
        return jax.random.normal(k, shape, f32) * scale

    page_table = jax.random.permutation(ks[6], n_pool)[:DEC_BATCH * n_pages].reshape(DEC_BATCH, n_pages).astype(jnp.int32)
    return {
        'x_prompt': nrm(ks[0], (BATCH, SEQ, D_MODEL), 1.0),
        'x_sample': nrm(ks[1], (DEC_BATCH, DEC_SEQ, D_MODEL), 1.0),
        'state_ret': nrm(ks[2], (DEPTH, DEC_BATCH, RET_HEADS, RET_DIM, RET_DIM), 0.5),
        'state_conv': nrm(ks[3], (DEPTH, DEC_BATCH, CONV_K - 1, CONV_WIDTH), 0.5),
        'cache_k': nrm(ks[4], (DEPTH, n_pool, PAGE_SIZE, SB_HEADS, SB_DIM), 1.0),
        'cache_v': nrm(ks[5], (DEPTH, n_pool, PAGE_SIZE, SB_HEADS, SB_DIM), 1.0),
        'page_table': page_table,
        'norm_mix': 1.0 + nrm(ks[7], (DEPTH, D_MODEL), 0.02),
        'w_in': nrm(ks[8], (DEPTH, D_MODEL, IN_WIDTH), D_MODEL ** -0.5),
        'ret_gn_gain': 1.0 + nrm(ks[9], (DEPTH, RET_WIDTH), 0.02),
        'ret_gn_bias': nrm(ks[10], (DEPTH, RET_WIDTH), 0.02),
        'conv_w': nrm(ks[11], (DEPTH, CONV_K, CONV_WIDTH), CONV_K ** -0.5),
        'conv_b': nrm(ks[12], (DEPTH, CONV_WIDTH), 0.02),
        'conv_ln_gain': 1.0 + nrm(ks[13], (DEPTH, CONV_WIDTH), 0.02),
        'conv_ln_bias': nrm(ks[14], (DEPTH, CONV_WIDTH), 0.02),
        'sb_bias': SB_BIAS_INIT + nrm(ks[25], (DEPTH, SB_HEADS), 0.3),
        'w_out': nrm(ks[15], (DEPTH, MIX_WIDTH, D_MODEL), MIX_WIDTH ** -0.5),
        'norm_ffn': 1.0 + nrm(ks[16], (DEPTH, D_MODEL), 0.02),
        'router_group_w': nrm(ks[17], (DEPTH, D_MODEL, N_GROUPS), D_MODEL ** -0.5),
        'router_group_b': nrm(ks[18], (DEPTH, N_GROUPS), 0.01),
        'router_expert_w': nrm(ks[19], (DEPTH, D_MODEL, N_EXPERTS), D_MODEL ** -0.5),
        'router_expert_b': nrm(ks[20], (DEPTH, N_EXPERTS), 0.01),
        'w_gate': nrm(ks[21], (DEPTH, N_EXPERTS, D_MODEL, D_EXPERT), D_MODEL ** -0.5),
        'w_up': nrm(ks[22], (DEPTH, N_EXPERTS, D_MODEL, D_EXPERT), D_MODEL ** -0.5),
        'w_down': nrm(ks[23], (DEPTH, N_EXPERTS, D_EXPERT, D_MODEL), D_EXPERT ** -0.5),
        'norm_final': 1.0 + nrm(ks[24], (D_MODEL,), 0.02),
    }


def reference(x_prompt, x_sample, state_ret, state_conv, cache_k, cache_v, page_table,
              norm_mix, w_in, ret_gn_gain, ret_gn_bias, conv_w, conv_b, conv_ln_gain, conv_ln_bias, sb_bias, w_out,
              norm_ffn, router_group_w, router_group_b, router_expert_w, router_expert_b, w_gate, w_up, w_down,
              norm_final):
    B = x_prompt.shape[0]
    DB = x_sample.shape[0]
    n_past = page_table.shape[1] * PAGE_SIZE
    dt = x_prompt.dtype
    xp, xs = x_prompt, x_sample
    p_ret, p_conv, p_k, p_v = [], [], [], []
    s_ret, s_conv, s_k, s_v = [], [], [], []
    for l in range(DEPTH):
        lp = (norm_mix[l], w_in[l], ret_gn_gain[l], ret_gn_bias[l], conv_w[l], conv_b[l], conv_ln_gain[l],
              conv_ln_bias[l], sb_bias[l], w_out[l], norm_ffn[l], router_group_w[l], router_group_b[l],
              router_expert_w[l], router_expert_b[l], w_gate[l], w_up[l], w_down[l])
        xp, r, cb, kn, vn = trunk_layer(
            xp, 0,
            jnp.zeros((B, RET_HEADS, RET_DIM, RET_DIM), dt),
            jnp.zeros((B, CONV_K - 1, CONV_WIDTH), dt),
            jnp.zeros((B, 0, SB_HEADS, SB_DIM), dt),
            jnp.zeros((B, 0, SB_HEADS, SB_DIM), dt), *lp)
        p_ret.append(r); p_conv.append(cb); p_k.append(kn); p_v.append(vn)
        k_past = cache_k[l][page_table].reshape(DB, n_past, SB_HEADS, SB_DIM)
        v_past = cache_v[l][page_table].reshape(DB, n_past, SB_HEADS, SB_DIM)
        xs, r, cb, kn, vn = trunk_layer(xs, n_past, state_ret[l], state_conv[l], k_past, v_past, *lp)
        s_ret.append(r); s_conv.append(cb); s_k.append(kn); s_v.append(vn)
    y_prompt = rms_norm(xp, norm_final)
    y_sample = rms_norm(xs, norm_final)
    return (y_prompt, y_sample,
            jnp.stack(p_ret), jnp.stack(p_conv), jnp.stack(p_k), jnp.stack(p_v),
            jnp.stack(s_ret), jnp.stack(s_conv), jnp.stack(s_k), jnp.stack(s_v))
```

```python
import functools

import numpy as np
import jax
import jax.numpy as jnp
from jax import lax
from jax.experimental import pallas as pl
from jax.experimental.pallas import tpu as pltpu

F32 = jnp.float32
BF16 = jnp.bfloat16
I32 = jnp.int32

D_MODEL = 2048
RET_HEADS = 4
RET_DIM = 128
RET_WIDTH = RET_HEADS * RET_DIM
RET_CHUNK = 128
ROPE_BASE = 10000.0
CONV_WIDTH = 512
CONV_K = 31
SB_HEADS = 8
SB_DIM = 128
SB_WIDTH = SB_HEADS * SB_DIM
IN_WIDTH = 4 * RET_WIDTH + 2 * CONV_WIDTH + 3 * SB_WIDTH
N_GROUPS = 4
EXPERTS_PER_GROUP = 8
N_EXPERTS = N_GROUPS * EXPERTS_PER_GROUP
TOP_K = 2
D_EXPERT = 512
PAGE_SIZE = 128
EPS = 1e-6

COL_RQ, COL_RK, COL_RV, COL_RG = 0, 512, 1024, 1536
COL_CA, COL_CG = 2048, 2560
COL_SQ, COL_SK, COL_SV = 3072, 4096, 5120

LANES = 128
SUBLANES = 8
SAMPLE_T_PAD = SUBLANES
CONV_HALO = 32
VMEM_LIMIT = 56 * 1024 * 1024

NT_DIMS = (((1,), (1,)), ((), ()))
TN_DIMS = (((0,), (0,)), ((), ()))


def _params(*sem):
    return pltpu.CompilerParams(dimension_semantics=sem, vmem_limit_bytes=VMEM_LIMIT)


def _norm_proj_body(x_ref, g_ref, w_ref, z_ref, h_ref):
    @pl.when(pl.program_id(1) == 0)
    def _():
        x = x_ref[...]
        ms = jnp.mean(x * x, axis=-1, keepdims=True)
        h_ref[...] = (x * lax.rsqrt(ms + EPS) * g_ref[...]).astype(BF16)

    z_ref[...] = jnp.dot(h_ref[...], w_ref[...], preferred_element_type=F32)


def _norm_proj(x, gain, w_bf16, tm, tn=512):
    n, d = x.shape
    dout = w_bf16.shape[1]
    return pl.pallas_call(
        _norm_proj_body,
        grid=(n // tm, dout // tn),
        in_specs=[
            pl.BlockSpec((tm, d), lambda i, j: (i, 0)),
            pl.BlockSpec((1, d), lambda i, j: (0, 0)),
            pl.BlockSpec((d, tn), lambda i, j: (0, j)),
        ],
        out_specs=pl.BlockSpec((tm, tn), lambda i, j: (i, j)),
        out_shape=jax.ShapeDtypeStruct((n, dout), F32),
        scratch_shapes=[pltpu.VMEM((tm, d), BF16)],
        compiler_params=_params("arbitrary", "arbitrary"),
        name="norm_proj",
    )(x, gain.reshape(1, d), w_bf16)


def _rotary_tables(positions):
    half = RET_DIM // 2
    inv = np.float32(ROPE_BASE) ** (-(np.arange(half, dtype=np.float32) / np.float32(half)))
    ang = positions.astype(np.float32)[:, None] * inv[None, :].astype(np.float32)
    cos = np.cos(ang).astype(np.float32)
    sin = np.sin(ang).astype(np.float32)
    return np.concatenate([cos, cos], axis=1), np.concatenate([-sin, sin], axis=1)


def _decay_tables(n_real):
    c = RET_CHUNK
    log_gamma = np.log(np.float32(1.0) - np.float32(2.0) ** (-5.0 - np.arange(RET_HEADS, dtype=np.float32))).astype(np.float32)
    i = np.arange(c, dtype=np.float32)
    diff = i[:, None] - i[None, :]
    real = (np.arange(c) < n_real)
    dmat = np.where(diff >= 0, np.exp(log_gamma[:, None, None] * np.maximum(diff, 0.0)), 0.0)
    dmat = dmat * (real[:, None] & real[None, :])[None]
    q_decay = np.exp(log_gamma[:, None] * (i[None, :] + 1.0)) * real[None, :]
    k_decay = np.exp(log_gamma[:, None] * (n_real - 1.0 - i[None, :])) * real[None, :]
    c_decay = np.exp(log_gamma * n_real)
    rep = lambda t: np.broadcast_to(t[:, :, None], (RET_HEADS, c, LANES)).astype(np.float32)
    cd = np.broadcast_to(c_decay[:, None, None], (RET_HEADS, SUBLANES, LANES)).astype(np.float32)
    return dmat.astype(np.float32), rep(q_decay), rep(k_decay), cd


def _retention_body(q_ref, k_ref, v_ref, g_ref, cos_ref, sin_ref, dm_ref, qd_ref, kd_ref, cd_ref,
                    gg_ref, gb_ref, s0_ref, o_ref, s_ref, *, rows):
    @pl.when(pl.program_id(2) == 0)
    def _():
        s_ref[...] = s0_ref[...]

    cos = cos_ref[...]
    sin = sin_ref[...]

    def rot(x):
        return x * cos + pltpu.roll(x, RET_DIM // 2, 1) * sin

    def pad(x):
        if rows == RET_CHUNK:
            return x
        return jnp.concatenate([x, jnp.zeros((RET_CHUNK - rows, x.shape[1]), x.dtype)], axis=0)

    q = pad(rot(q_ref[...]))
    k = pad(rot(k_ref[...]) * (RET_DIM ** -0.5))
    v = pad(v_ref[...])
    qb = q.astype(BF16)
    kb = k.astype(BF16)
    vb = v.astype(BF16)
    s = s_ref[0, 0]
    att = lax.dot_general(qb, kb, NT_DIMS, preferred_element_type=F32) * dm_ref[0]
    o = jnp.dot(att.astype(BF16), vb, preferred_element_type=F32)
    o = o + jnp.dot(qb, s.astype(BF16), preferred_element_type=F32) * qd_ref[0]
    kdec = (k * kd_ref[0]).astype(BF16)
    s_ref[0, 0] = s * cd_ref[0][0:1, :] + lax.dot_general(kdec, vb, TN_DIMS, preferred_element_type=F32)

    o = o[0:rows]
    mu = jnp.mean(o, axis=-1, keepdims=True)
    d = o - mu
    var = jnp.mean(d * d, axis=-1, keepdims=True)
    y = d * lax.rsqrt(var + EPS) * gg_ref[0] + gb_ref[0]
    g = g_ref[...]
    o_ref[...] = (g * jax.nn.sigmoid(g) * y).astype(BF16)


def _retention(z, s0, gn_gain, gn_bias, batch, t_pad, n_real, q_start):
    rows = min(t_pad, RET_CHUNK)
    nc = t_pad // rows
    assert n_real == t_pad or nc == 1, "padding rows are only supported in a single-chunk sequence"
    cos2, sin2 = _rotary_tables(q_start + np.arange(t_pad))
    dmat, qd, kd, cd = _decay_tables(min(n_real, rows))
    hb = RET_DIM
    col = lambda base: (lambda b, h, c: (b * nc + c, base // hb + h))
    tab = lambda b, h, c: (h, 0, 0)
    o_r, s_new = pl.pallas_call(
        functools.partial(_retention_body, rows=rows),
        grid=(batch, RET_HEADS, nc),
        in_specs=[
            pl.BlockSpec((rows, hb), col(COL_RQ)),
            pl.BlockSpec((rows, hb), col(COL_RK)),
            pl.BlockSpec((rows, hb), col(COL_RV)),
            pl.BlockSpec((rows, hb), col(COL_RG)),
            pl.BlockSpec((rows, hb), lambda b, h, c: (c, 0)),
            pl.BlockSpec((rows, hb), lambda b, h, c: (c, 0)),
            pl.BlockSpec((1, RET_CHUNK, RET_CHUNK), tab),
            pl.BlockSpec((1, RET_CHUNK, LANES), tab),
            pl.BlockSpec((1, RET_CHUNK, LANES), tab),
            pl.BlockSpec((1, SUBLANES, LANES), tab),
            pl.BlockSpec((1, 1, hb), tab),
            pl.BlockSpec((1, 1, hb), tab),
            pl.BlockSpec((1, 1, hb, hb), lambda b, h, c: (b, h, 0, 0)),
        ],
        out_specs=[
            pl.BlockSpec((rows, hb), lambda b, h, c: (b * nc + c, h)),
            pl.BlockSpec((1, 1, hb, hb), lambda b, h, c: (b, h, 0, 0)),
        ],
        out_shape=[
            jax.ShapeDtypeStruct((batch * t_pad, RET_WIDTH), BF16),
            jax.ShapeDtypeStruct((batch, RET_HEADS, hb, hb), F32),
        ],
        compiler_params=_params("arbitrary", "arbitrary", "arbitrary"),
        name="retention",
    )(z, z, z, z, jnp.asarray(cos2), jnp.asarray(sin2), jnp.asarray(dmat), jnp.asarray(qd), jnp.asarray(kd),
      jnp.asarray(cd), gn_gain.reshape(RET_HEADS, 1, hb), gn_bias.reshape(RET_HEADS, 1, hb), s0)
    return o_r, s_new


def _conv_body(a_ref, g_ref, buf_ref, w_ref, b_ref, lg_ref, lb_ref, o_ref, nb_ref, ub_ref, *, tt, n_last):
    t = pl.program_id(1)

    @pl.when(t == 0)
    def _():
        ub_ref[0:CONV_HALO, :] = buf_ref[0]

    g = g_ref[...]
    ub_ref[CONV_HALO:CONV_HALO + tt, :] = a_ref[...] * jax.nn.sigmoid(g)

    first = CONV_HALO - (CONV_K - 1)
    acc = jnp.broadcast_to(b_ref[...], (tt, CONV_WIDTH))
    for k in range(CONV_K):
        acc = acc + w_ref[k:k + 1, :] * ub_ref[first + k:first + k + tt, :]

    mu = jnp.mean(acc, axis=-1, keepdims=True)
    d = acc - mu
    var = jnp.mean(d * d, axis=-1, keepdims=True)
    y = d * lax.rsqrt(var + EPS) * lg_ref[...] + lb_ref[...]
    o_ref[...] = (y * jax.nn.sigmoid(y)).astype(BF16)

    @pl.when(t == pl.num_programs(1) - 1)
    def _():
        nb_ref[0] = ub_ref[n_last:n_last + CONV_HALO, :]

    ub_ref[0:CONV_HALO, :] = ub_ref[tt:tt + CONV_HALO, :]


def _conv(z, buf, w, b, ln_g, ln_b, batch, t_pad, n_real):
    tt = min(t_pad, 512)
    nt = t_pad // tt
    n_last = n_real - (nt - 1) * tt
    first = CONV_HALO - (CONV_K - 1)
    buf_p = jnp.pad(buf, ((0, 0), (first, 0), (0, 0)))
    w_p = jnp.pad(w, ((0, CONV_HALO - CONV_K), (0, 0)))
    cw = CONV_WIDTH
    vec = lambda v: v.reshape(1, cw)
    o_c, nb = pl.pallas_call(
        functools.partial(_conv_body, tt=tt, n_last=n_last),
        grid=(batch, nt),
        in_specs=[
            pl.BlockSpec((tt, cw), lambda bb, t: (bb * nt + t, COL_CA // cw)),
            pl.BlockSpec((tt, cw), lambda bb, t: (bb * nt + t, COL_CG // cw)),
            pl.BlockSpec((1, CONV_HALO, cw), lambda bb, t: (bb, 0, 0)),
            pl.BlockSpec((CONV_HALO, cw), lambda bb, t: (0, 0)),
            pl.BlockSpec((1, cw), lambda bb, t: (0, 0)),
            pl.BlockSpec((1, cw), lambda bb, t: (0, 0)),
            pl.BlockSpec((1, cw), lambda bb, t: (0, 0)),
        ],
        out_specs=[
            pl.BlockSpec((tt, cw), lambda bb, t: (bb * nt + t, 0)),
            pl.BlockSpec((1, CONV_HALO, cw), lambda bb, t: (bb, 0, 0)),
        ],
        out_shape=[
            jax.ShapeDtypeStruct((batch * t_pad, cw), BF16),
            jax.ShapeDtypeStruct((batch, CONV_HALO, cw), F32),
        ],
        scratch_shapes=[pltpu.VMEM((CONV_HALO + tt, cw), F32)],
        compiler_params=_params("arbitrary", "arbitrary"),
        name="conv",
    )(z, z, buf_p, w_p, vec(b), vec(ln_g), vec(ln_b))
    return o_c, nb[:, first:, :]


def _suffix_matrix(bk):
    j = np.arange(bk)[:, None]
    s = np.arange(bk)[None, :]
    m = np.concatenate([(j > s).astype(np.float32), np.ones((bk, LANES), np.float32)], axis=1)
    return jnp.asarray(m, dtype=BF16)


def _sb_tile(q, kblk, vblk, u, bias, c, acc, masked):
    bq = q.shape[0]
    bk = kblk.shape[0]
    s = lax.dot_general(q, kblk, NT_DIMS, preferred_element_type=F32)
    z = s * (SB_DIM ** -0.5) + bias
    sp = jnp.maximum(z, 0.0) + jnp.log(1.0 + jnp.exp(-jnp.abs(z)))
    neg = -sp
    if masked:
        row = lax.broadcasted_iota(I32, (bq, bk), 0)
        colk = lax.broadcasted_iota(I32, (bq, bk), 1)
        causal = colk < row
        neg = jnp.where(causal, neg, 0.0)
    rt = jnp.dot(neg.astype(BF16), u, preferred_element_type=F32)
    suffix = rt[:, :bk]
    total = rt[:, bk:]
    c_all = c if bk == LANES else jnp.concatenate([c] * (bk // LANES), axis=1)
    a = jnp.exp((z - sp) + c_all + suffix)
    if masked:
        a = jnp.where(causal, a, 0.0)
    acc = acc + jnp.dot(a.astype(BF16), vblk, preferred_element_type=F32)
    return c + total, acc


def _sb_prompt_body(bias_ref, q_ref, k_ref, v_ref, u_ref, o_ref, kb_ref, vb_ref, *, bq):
    h = pl.program_id(1)
    i = pl.program_id(2)

    @pl.when(i == 0)
    def _():
        kb_ref[...] = k_ref[...].astype(BF16)
        vb_ref[...] = v_ref[...].astype(BF16)

    q = q_ref[...].astype(BF16)
    bias = bias_ref[h]
    u = u_ref[...]
    zero = jnp.zeros((bq, LANES), F32)
    start = pl.multiple_of(i * bq, bq)
    c, acc = _sb_tile(q, kb_ref[pl.ds(start, bq), :], vb_ref[pl.ds(start, bq), :], u, bias, zero, zero, True)

    def body(n, carry):
        st = pl.multiple_of((i - 1 - n) * bq, bq)
        return _sb_tile(q, kb_ref[pl.ds(st, bq), :], vb_ref[pl.ds(st, bq), :], u, bias, carry[0], carry[1], False)

    c, acc = lax.fori_loop(0, i, body, (c, acc))
    o_ref[...] = acc.astype(BF16)


def _sb_prompt(z, sb_bias, batch, t, bq=256):
    nq = t // bq
    hb = SB_DIM
    return pl.pallas_call(
        functools.partial(_sb_prompt_body, bq=bq),
        grid=(batch, SB_HEADS, nq),
        in_specs=[
            pl.BlockSpec(memory_space=pltpu.SMEM),
            pl.BlockSpec((bq, hb), lambda b, h, i: (b * nq + i, COL_SQ // hb + h)),
            pl.BlockSpec((t, hb), lambda b, h, i: (b, COL_SK // hb + h)),
            pl.BlockSpec((t, hb), lambda b, h, i: (b, COL_SV // hb + h)),
            pl.BlockSpec((bq, bq + LANES), lambda b, h, i: (0, 0)),
        ],
        out_specs=pl.BlockSpec((bq, hb), lambda b, h, i: (b * nq + i, h)),
        out_shape=jax.ShapeDtypeStruct((batch * t, SB_WIDTH), BF16),
        scratch_shapes=[pltpu.VMEM((t, hb), BF16), pltpu.VMEM((t, hb), BF16)],
        compiler_params=_params("arbitrary", "arbitrary", "arbitrary"),
        name="sb_prompt",
    )(sb_bias, z, z, z, _suffix_matrix(bq))


def _sb_sample_body(pt_ref, bias_ref, q_ref, kn_ref, vn_ref, kp_ref, vp_ref, u_ref, o_ref, c_ref, acc_ref, *, n_pages):
    p = pl.program_id(1)
    u = u_ref[...]

    def process(head_k, head_v, first):
        for h in range(SB_HEADS):
            q = q_ref[:, h * SB_DIM:(h + 1) * SB_DIM].astype(BF16)
            kblk = head_k(h).astype(BF16)
            vblk = head_v(h).astype(BF16)
            if first:
                c = jnp.zeros((SAMPLE_T_PAD, LANES), F32)
                acc = jnp.zeros((SAMPLE_T_PAD, LANES), F32)
            else:
                c = c_ref[h]
                acc = acc_ref[h]
            c, acc = _sb_tile(q, kblk, vblk, u, bias_ref[h], c, acc, first)
            c_ref[h] = c
            acc_ref[h] = acc

    @pl.when(p == 0)
    def _():
        process(lambda h: kn_ref[0, :, h * SB_DIM:(h + 1) * SB_DIM],
                lambda h: vn_ref[0, :, h * SB_DIM:(h + 1) * SB_DIM], True)

    @pl.when(p > 0)
    def _():
        process(lambda h: kp_ref[0, 0, :, h, :], lambda h: vp_ref[0, 0, :, h, :], False)

    @pl.when(p == n_pages)
    def _():
        for h in range(SB_HEADS):
            o_ref[:, h * SB_DIM:(h + 1) * SB_DIM] = acc_ref[h].astype(BF16)


def _sb_sample(z, cache_k, cache_v, layer, page_table, sb_bias, batch):
    n_pages = page_table.shape[1]
    tp = SAMPLE_T_PAD
    w = SB_WIDTH

    def new_page(col):
        rows = z[:, col:col + w].reshape(batch, tp, w)
        return jnp.pad(rows, ((0, 0), (0, PAGE_SIZE - tp), (0, 0)))

    page = lambda b, p, pt: (layer, pt[b, n_pages - jnp.maximum(p, 1)], 0, 0, 0)
    grid_spec = pltpu.PrefetchScalarGridSpec(
        num_scalar_prefetch=1,
        grid=(batch, n_pages + 1),
        in_specs=[
            pl.BlockSpec(memory_space=pltpu.SMEM),
            pl.BlockSpec((tp, w), lambda b, p, pt: (b, COL_SQ // w)),
            pl.BlockSpec((1, PAGE_SIZE, w), lambda b, p, pt: (b, 0, 0)),
            pl.BlockSpec((1, PAGE_SIZE, w), lambda b, p, pt: (b, 0, 0)),
            pl.BlockSpec((1, 1, PAGE_SIZE, SB_HEADS, SB_DIM), page),
            pl.BlockSpec((1, 1, PAGE_SIZE, SB_HEADS, SB_DIM), page),
            pl.BlockSpec((PAGE_SIZE, PAGE_SIZE + LANES), lambda b, p, pt: (0, 0)),
        ],
        out_specs=pl.BlockSpec((tp, w), lambda b, p, pt: (b, 0)),
        scratch_shapes=[pltpu.VMEM((SB_HEADS, tp, LANES), F32), pltpu.VMEM((SB_HEADS, tp, LANES), F32)],
    )
    return pl.pallas_call(
        functools.partial(_sb_sample_body, n_pages=n_pages),
        grid_spec=grid_spec,
        out_shape=jax.ShapeDtypeStruct((batch * tp, w), BF16),
        compiler_params=_params("arbitrary", "arbitrary"),
        name="sb_sample",
    )(page_table, sb_bias, z, new_page(COL_SK), new_page(COL_SV), cache_k, cache_v, _suffix_matrix(PAGE_SIZE))


ROUTE_ROWS = SUBLANES
ROUTER_PAD = LANES


def _route_rows(lt):
    lg = [lt[g:g + 1, :] for g in range(N_GROUPS)]
    m = functools.reduce(jnp.maximum, lg)
    gi = jnp.full(m.shape, N_GROUPS - 1, I32)
    for g in range(N_GROUPS - 2, -1, -1):
        gi = jnp.where(lg[g] == m, g, gi)
    den = functools.reduce(jnp.add, [jnp.exp(x - m) for x in lg])
    p_group = 1.0 / den

    def expert_logit(k):
        out = lt[N_GROUPS + (N_GROUPS - 1) * EXPERTS_PER_GROUP + k:N_GROUPS + (N_GROUPS - 1) * EXPERTS_PER_GROUP + k + 1, :]
        for g in range(N_GROUPS - 2, -1, -1):
            r = N_GROUPS + g * EXPERTS_PER_GROUP + k
            out = jnp.where(gi == g, lt[r:r + 1, :], out)
        return out

    le = [expert_logit(k) for k in range(EXPERTS_PER_GROUP)]
    v1 = functools.reduce(jnp.maximum, le)
    i1 = jnp.full(m.shape, EXPERTS_PER_GROUP - 1, I32)
    for k in range(EXPERTS_PER_GROUP - 2, -1, -1):
        i1 = jnp.where(le[k] == v1, k, i1)
    rest = [jnp.where(i1 == k, -jnp.inf, le[k]) for k in range(EXPERTS_PER_GROUP)]
    v2 = functools.reduce(jnp.maximum, rest)
    i2 = jnp.full(m.shape, EXPERTS_PER_GROUP - 1, I32)
    for k in range(EXPERTS_PER_GROUP - 2, -1, -1):
        i2 = jnp.where(rest[k] == v2, k, i2)
    e2 = jnp.exp(v2 - v1)
    inv = 1.0 / (1.0 + e2)
    rows = [p_group * inv, p_group * (e2 * inv),
            (gi * EXPERTS_PER_GROUP + i1).astype(F32), (gi * EXPERTS_PER_GROUP + i2).astype(F32)]
    rows += [jnp.zeros_like(m)] * (ROUTE_ROWS - len(rows))
    return jnp.concatenate(rows, axis=0)


def _out_proj_body(x_ref, or_ref, oc_ref, os_ref, w_ref, g_ref, wrh_ref, wrl_ref, br_ref, x2_ref, h2_ref, rt_ref):
    y = jnp.dot(or_ref[...], w_ref[0:RET_WIDTH, :], preferred_element_type=F32)
    y = y + jnp.dot(oc_ref[...], w_ref[RET_WIDTH:RET_WIDTH + CONV_WIDTH, :], preferred_element_type=F32)
    y = y + jnp.dot(os_ref[...], w_ref[RET_WIDTH + CONV_WIDTH:, :], preferred_element_type=F32)
    x2 = x_ref[...] + y
    x2_ref[...] = x2
    ms = jnp.mean(x2 * x2, axis=-1, keepdims=True)
    h = x2 * lax.rsqrt(ms + EPS) * g_ref[...]
    h2_ref[...] = h
    hh = h.astype(BF16)
    hl = (h - hh.astype(F32)).astype(BF16)
    wh = wrh_ref[...]
    lt = lax.dot_general(wh, hh, NT_DIMS, preferred_element_type=F32)
    lt = lt + lax.dot_general(wh, hl, NT_DIMS, preferred_element_type=F32)
    lt = lt + lax.dot_general(wrl_ref[...], hh, NT_DIMS, preferred_element_type=F32)
    rt_ref[...] = _route_rows(lt + br_ref[...])


def _out_proj(x, o_r, o_c, o_s, w_out_bf16, norm_ffn, rg_w, rg_b, re_w, re_b, tm):
    n, d = x.shape
    wr = jnp.concatenate([rg_w, re_w], axis=1).T
    wr = jnp.pad(wr, ((0, ROUTER_PAD - wr.shape[0]), (0, 0)))
    wr_hi = wr.astype(BF16)
    wr_lo = (wr - wr_hi.astype(F32)).astype(BF16)
    br = jnp.pad(jnp.concatenate([rg_b, re_b]), (0, ROUTER_PAD - N_GROUPS - N_EXPERTS)).reshape(ROUTER_PAD, 1)
    row = lambda width: pl.BlockSpec((tm, width), lambda i: (i, 0))
    full = lambda a, b: pl.BlockSpec((a, b), lambda i: (0, 0))
    return pl.pallas_call(
        _out_proj_body,
        grid=(n // tm,),
        in_specs=[row(d), row(RET_WIDTH), row(CONV_WIDTH), row(SB_WIDTH), full(d, d), full(1, d),
                  full(ROUTER_PAD, d), full(ROUTER_PAD, d), full(ROUTER_PAD, 1)],
        out_specs=[row(d), row(d), pl.BlockSpec((ROUTE_ROWS, tm), lambda i: (0, i))],
        out_shape=[jax.ShapeDtypeStruct((n, d), F32), jax.ShapeDtypeStruct((n, d), F32),
                   jax.ShapeDtypeStruct((ROUTE_ROWS, n), F32)],
        compiler_params=_params("arbitrary"),
        name="out_proj",
    )(x, o_r, o_c, o_s, w_out_bf16, norm_ffn.reshape(1, d), wr_hi, wr_lo, br)


def _moe_plan(route, n_tok, bm):
    gates = route[0:TOP_K, :]
    experts = route[TOP_K:2 * TOP_K, :].astype(I32)
    a = TOP_K * n_tok
    n_blocks = (a + N_EXPERTS * (bm - 1) + bm - 1) // bm
    r = n_blocks * bm
    flat_e = experts.reshape(-1)
    onehot = (flat_e[:, None] == jnp.arange(N_EXPERTS, dtype=I32)[None, :]).astype(I32)
    rank = jnp.sum((jnp.cumsum(onehot, axis=0) - onehot) * onehot, axis=1)
    counts = jnp.sum(onehot, axis=0)
    padded = (counts + bm - 1) // bm * bm
    pad_ends = jnp.cumsum(padded)
    pad_starts = pad_ends - padded
    dest = pad_starts[flat_e] + rank
    slot_of_row = jnp.full((r,), -1, I32).at[dest].set(jnp.arange(a, dtype=I32))
    is_pad = slot_of_row < 0
    row_dst = jnp.where(is_pad, a + jnp.arange(r, dtype=I32), slot_of_row)
    row_gate = jnp.where(is_pad, 0.0, gates.reshape(-1)[jnp.maximum(slot_of_row, 0)])
    n_used = (pad_ends[-1] // bm).astype(I32)
    blk = jnp.minimum(jnp.arange(n_blocks, dtype=I32), n_used - 1) * bm
    blk_expert = jnp.minimum(jnp.searchsorted(pad_ends, blk, side='right'), N_EXPERTS - 1).astype(I32)
    gate_rows = jnp.broadcast_to(row_gate[:, None], (r, LANES))
    return row_dst, blk_expert, n_used.reshape(1), gate_rows, n_blocks


def _moe_body(dst_ref, be_ref, nu_ref, h_hbm, gate_ref, wg_ref, wu_ref, wd_ref, y_hbm,
              xbuf, ybuf, dump, gsem, ssem, *, bm, n_tok):
    i = pl.program_id(0)
    n_used = nu_ref[0]
    slot = i % 2

    def gather_copy(row, tok, s):
        return pltpu.make_async_copy(h_hbm.at[pl.ds(tok, 1), :], xbuf.at[s, pl.ds(row, 1), :], gsem.at[s])

    def scatter_copy(row, dst, s):
        return pltpu.make_async_copy(ybuf.at[s, pl.ds(row, 1), :], y_hbm.at[pl.ds(dst, 1), :], ssem.at[s])

    def start_gather(blk, s):
        def body(j, carry):
            d = dst_ref[blk * bm + j]
            tok = jnp.where(d >= n_tok, d - n_tok, d)
            tok = jnp.where(d >= TOP_K * n_tok, 0, tok)
            gather_copy(j, tok, s).start()
            return carry
        lax.fori_loop(0, bm, body, 0)

    def wait_gather(s):
        pltpu.make_async_copy(h_hbm.at[pl.ds(0, bm), :], xbuf.at[s], gsem.at[s]).wait()

    def start_scatter(blk, s):
        def body(j, carry):
            d = dst_ref[blk * bm + j]

            @pl.when(d < TOP_K * n_tok)
            def _():
                scatter_copy(j, d, s).start()

            @pl.when(d >= TOP_K * n_tok)
            def _():
                pltpu.make_async_copy(ybuf.at[s, pl.ds(j, 1), :], dump.at[s, pl.ds(j, 1), :], ssem.at[s]).start()

            return carry
        lax.fori_loop(0, bm, body, 0)

    def wait_scatter(s):
        pltpu.make_async_copy(ybuf.at[s], y_hbm.at[pl.ds(0, bm), :], ssem.at[s]).wait()

    @pl.when(i == 0)
    def _():
        start_gather(0, 0)

    @pl.when(i + 1 < n_used)
    def _():
        start_gather(i + 1, 1 - slot)

    @pl.when(i < n_used)
    def _():
        wait_gather(slot)

        @pl.when(i >= 2)
        def _():
            wait_scatter(slot)

        x = xbuf[slot].astype(BF16)
        g = jnp.dot(x, wg_ref[0], preferred_element_type=F32)
        u = jnp.dot(x, wu_ref[0], preferred_element_type=F32)
        hm = (g * jax.nn.sigmoid(g) * u).astype(BF16)
        y = jnp.dot(hm, wd_ref[0], preferred_element_type=F32)
        gate = gate_ref[...]
        ybuf[slot] = y * jnp.concatenate([gate] * (D_MODEL // LANES), axis=1)
        start_scatter(i, slot)

    @pl.when(i == pl.num_programs(0) - 1)
    def _():
        wait_scatter((n_used - 1) % 2)

        @pl.when(n_used >= 2)
        def _():
            wait_scatter(n_used % 2)


def _moe(h2, route, wg_bf16, wu_bf16, wd_bf16, bm):
    n_tok, d = h2.shape
    row_dst, blk_expert, n_used, gate_rows, n_blocks = _moe_plan(route, n_tok, bm)
    wspec = lambda a, b: pl.BlockSpec((1, a, b), lambda i, dst, be, nu: (be[i], 0, 0))
    grid_spec = pltpu.PrefetchScalarGridSpec(
        num_scalar_prefetch=3,
        grid=(n_blocks,),
        in_specs=[
            pl.BlockSpec(memory_space=pl.ANY),
            pl.BlockSpec((bm, LANES), lambda i, dst, be, nu: (i, 0)),
            wspec(d, D_EXPERT), wspec(d, D_EXPERT), wspec(D_EXPERT, d),
        ],
        out_specs=pl.BlockSpec(memory_space=pl.ANY),
        scratch_shapes=[pltpu.VMEM((2, bm, d), F32), pltpu.VMEM((2, bm, d), F32), pltpu.VMEM((2, bm, d), F32),
                        pltpu.SemaphoreType.DMA((2,)), pltpu.SemaphoreType.DMA((2,))],
    )
    return pl.pallas_call(
        functools.partial(_moe_body, bm=bm, n_tok=n_tok),
        grid_spec=grid_spec,
        out_shape=jax.ShapeDtypeStruct((TOP_K * n_tok, d), F32),
        compiler_params=_params("arbitrary"),
        name="moe",
    )(row_dst, blk_expert, n_used, h2, gate_rows, wg_bf16, wu_bf16, wd_bf16)


def _combine_body(x_ref, y0_ref, y1_ref, g_ref, o_ref, *, final_norm):
    x = x_ref[...] + y0_ref[...] + y1_ref[...]
    if final_norm:
        ms = jnp.mean(x * x, axis=-1, keepdims=True)
        x = x * lax.rsqrt(ms + EPS) * g_ref[...]
    o_ref[...] = x


def _combine(x2, y, norm_gain, final_norm, tm):
    n, d = x2.shape
    nb = n // tm
    return pl.pallas_call(
        functools.partial(_combine_body, final_norm=final_norm),
        grid=(nb,),
        in_specs=[
            pl.BlockSpec((tm, d), lambda i: (i, 0)),
            pl.BlockSpec((tm, d), lambda i: (i, 0)),
            pl.BlockSpec((tm, d), lambda i: (i + nb, 0)),
            pl.BlockSpec((1, d), lambda i: (0, 0)),
        ],
        out_specs=pl.BlockSpec((tm, d), lambda i: (i, 0)),
        out_shape=jax.ShapeDtypeStruct((n, d), F32),
        compiler_params=_params("arbitrary"),
        name="combine",
    )(x2, y, y, norm_gain.reshape(1, d))


def _layer(x, l, last, group, w, state):
    batch, t_pad, n_real = group['batch'], group['t_pad'], group['n_real']
    z = _norm_proj(x, w['norm_mix'][l], w['w_in'][l], group['tm'])
    o_r, s_ret = _retention(z, state['ret'], w['ret_gn_gain'][l], w['ret_gn_bias'][l], batch, t_pad, n_real, group['q_start'])
    o_c, new_buf = _conv(z, state['conv'], w['conv_w'][l], w['conv_b'][l], w['conv_ln_gain'][l], w['conv_ln_bias'][l],
                         batch, t_pad, n_real)
    if group['paged']:
        o_s = _sb_sample(z, state['cache_k'], state['cache_v'], l, state['page_table'], w['sb_bias'][l], batch)
    else:
        o_s = _sb_prompt(z, w['sb_bias'][l], batch, t_pad)
    x2, h2, route = _out_proj(x, o_r, o_c, o_s, w['w_out'][l], w['norm_ffn'][l], w['router_group_w'][l],
                              w['router_group_b'][l], w['router_expert_w'][l], w['router_expert_b'][l], group['tm_out'])
    y = _moe(h2, route, w['w_gate'][l], w['w_up'][l], w['w_down'][l], group['bm'])
    x3 = _combine(x2, y, w['norm_final'], last, group['tm_out'])
    k_new = z[:, COL_SK:COL_SK + SB_WIDTH].reshape(batch, t_pad, SB_HEADS, SB_DIM)[:, :n_real]
    v_new = z[:, COL_SV:COL_SV + SB_WIDTH].reshape(batch, t_pad, SB_HEADS, SB_DIM)[:, :n_real]
    return x3, s_ret, new_buf, k_new, v_new


def kernel(x_prompt, x_sample, state_ret, state_conv, cache_k, cache_v, page_table, norm_mix, w_in, ret_gn_gain, ret_gn_bias, conv_w, conv_b, conv_ln_gain, conv_ln_bias, sb_bias, w_out, norm_ffn, router_group_w, router_group_b, router_expert_w, router_expert_b, w_gate, w_up, w_down, norm_final):
    b, t, d = x_prompt.shape
    db, dt, _ = x_sample.shape
    depth = w_in.shape[0]
    n_past = page_table.shape[1] * PAGE_SIZE
    w = dict(norm_mix=norm_mix, w_in=w_in.astype(BF16), ret_gn_gain=ret_gn_gain, ret_gn_bias=ret_gn_bias,
             conv_w=conv_w, conv_b=conv_b, conv_ln_gain=conv_ln_gain, conv_ln_bias=conv_ln_bias, sb_bias=sb_bias,
             w_out=w_out.astype(BF16), norm_ffn=norm_ffn, router_group_w=router_group_w, router_group_b=router_group_b,
             router_expert_w=router_expert_w, router_expert_b=router_expert_b,
             w_gate=w_gate.astype(BF16), w_up=w_up.astype(BF16), w_down=w_down.astype(BF16), norm_final=norm_final)
    prompt = dict(batch=b, t_pad=t, n_real=t, q_start=0, tm=512, tm_out=256, bm=256, paged=False)
    sample = dict(batch=db, t_pad=SAMPLE_T_PAD, n_real=dt, q_start=n_past, tm=db * SAMPLE_T_PAD,
                  tm_out=db * SAMPLE_T_PAD, bm=SUBLANES, paged=True)

    xp = x_prompt.reshape(b * t, d)
    xs = jnp.pad(x_sample, ((0, 0), (0, SAMPLE_T_PAD - dt), (0, 0))).reshape(db * SAMPLE_T_PAD, d)
    outs_p, outs_s = [], []
    for l in range(depth):
        last = l == depth - 1
        st_p = dict(ret=jnp.zeros((b, RET_HEADS, RET_DIM, RET_DIM), F32), conv=jnp.zeros((b, CONV_K - 1, CONV_WIDTH), F32))
        xp, *lp = _layer(xp, l, last, prompt, w, st_p)
        outs_p.append(lp)
        st_s = dict(ret=state_ret[l], conv=state_conv[l], cache_k=cache_k, cache_v=cache_v, page_table=page_table)
        xs, *ls = _layer(xs, l, last, sample, w, st_s)
        outs_s.append(ls)
    y_prompt = xp.reshape(b, t, d)
    y_sample = xs.reshape(db, SAMPLE_T_PAD, d)[:, :dt]
    stack = lambda outs, j: jnp.stack([o[j] for o in outs])
    return (y_prompt, y_sample,
            stack(outs_p, 0), stack(outs_p, 1), stack(outs_p, 2), stack(outs_p, 3),
            stack(outs_s, 0), stack(outs_s, 1), stack(outs_s, 2), stack(outs_s, 3))
```

```python
import functools

import numpy as np
import jax
import jax.numpy as jnp
from jax import lax
from jax.experimental import pallas as pl
from jax.experimental.pallas import tpu as pltpu

F32 = jnp.float32
BF16 = jnp.bfloat16
I32 = jnp.int32

D_MODEL = 2048
RET_HEADS = 4
RET_DIM = 128
RET_WIDTH = RET_HEADS * RET_DIM
RET_CHUNK = 128
ROPE_BASE = 10000.0
CONV_WIDTH = 512
CONV_K = 31
SB_HEADS = 8
SB_DIM = 128
SB_WIDTH = SB_HEADS * SB_DIM
IN_WIDTH = 4 * RET_WIDTH + 2 * CONV_WIDTH + 3 * SB_WIDTH
N_GROUPS = 4
EXPERTS_PER_GROUP = 8
N_EXPERTS = N_GROUPS * EXPERTS_PER_GROUP
TOP_K = 2
D_EXPERT = 512
PAGE_SIZE = 128
EPS = 1e-6

COL_RQ, COL_RK, COL_RV, COL_RG = 0, 512, 1024, 1536
COL_CA, COL_CG = 2048, 2560
COL_SQ, COL_SK, COL_SV = 3072, 4096, 5120

LANES = 128
SUBLANES = 8
SAMPLE_T_PAD = SUBLANES
CONV_HALO = 32
VMEM_LIMIT = 56 * 1024 * 1024

NT_DIMS = (((1,), (1,)), ((), ()))
TN_DIMS = (((0,), (0,)), ((), ()))


def _params(*sem):
    return pltpu.CompilerParams(dimension_semantics=sem, vmem_limit_bytes=VMEM_LIMIT)


def _norm_proj_body(x_ref, g_ref, w_ref, z_ref, h_ref):
    @pl.when(pl.program_id(1) == 0)
    def _():
        x = x_ref[...]
        ms = jnp.mean(x * x, axis=-1, keepdims=True)
        h_ref[...] = (x * lax.rsqrt(ms + EPS) * g_ref[...]).astype(BF16)

    z_ref[...] = jnp.dot(h_ref[...], w_ref[...], preferred_element_type=F32)


def _norm_proj(x, gain, w_bf16, tm, tn=512):
    n, d = x.shape
    dout = w_bf16.shape[1]
    return pl.pallas_call(
        _norm_proj_body,
        grid=(n // tm, dout // tn),
        in_specs=[
            pl.BlockSpec((tm, d), lambda i, j: (i, 0)),
            pl.BlockSpec((1, d), lambda i, j: (0, 0)),
            pl.BlockSpec((d, tn), lambda i, j: (0, j)),
        ],
        out_specs=pl.BlockSpec((tm, tn), lambda i, j: (i, j)),
        out_shape=jax.ShapeDtypeStruct((n, dout), F32),
        scratch_shapes=[pltpu.VMEM((tm, d), BF16)],
        compiler_params=_params("arbitrary", "arbitrary"),
        name="norm_proj",
    )(x, gain.reshape(1, d), w_bf16)


def _rotary_tables(positions):
    half = RET_DIM // 2
    inv = ROPE_BASE ** (-jnp.arange(half, dtype=F32) / half)
    ang = positions.astype(F32)[:, None] * inv[None, :]
    cos = jnp.cos(ang)
    sin = jnp.sin(ang)
    return jnp.concatenate([cos, cos], axis=1), jnp.concatenate([-sin, sin], axis=1)


def _decay_tables(n_real):
    c = RET_CHUNK
    log_gamma = jnp.log(1.0 - 2.0 ** (-5.0 - jnp.arange(RET_HEADS, dtype=F32)))
    i = jnp.arange(c, dtype=F32)
    diff = i[:, None] - i[None, :]
    real = jnp.arange(c) < n_real
    dmat = jnp.where(diff >= 0, jnp.exp(log_gamma[:, None, None] * jnp.maximum(diff, 0.0)), 0.0)
    dmat = jnp.where((real[:, None] & real[None, :])[None], dmat, 0.0)
    q_decay = jnp.where(real[None, :], jnp.exp(log_gamma[:, None] * (i[None, :] + 1.0)), 0.0)
    k_decay = jnp.where(real[None, :], jnp.exp(log_gamma[:, None] * (n_real - 1.0 - i[None, :])), 0.0)
    c_decay = jnp.exp(log_gamma * n_real)
    rep = lambda t: jnp.broadcast_to(t[:, :, None], (RET_HEADS, c, LANES))
    cd = jnp.broadcast_to(c_decay[:, None, None], (RET_HEADS, SUBLANES, LANES))
    return dmat, rep(q_decay), rep(k_decay), cd


def _retention_body(q_ref, k_ref, v_ref, g_ref, cos_ref, sin_ref, dm_ref, qd_ref, kd_ref, cd_ref,
                    gg_ref, gb_ref, s0_ref, o_ref, s_ref, *, rows):
    @pl.when(pl.program_id(1) == 0)
    def _():
        s_ref[...] = s0_ref[...]

    cos = cos_ref[...]
    sin = sin_ref[...]

    def rot(x):
        return x * cos + pltpu.roll(x, RET_DIM // 2, 1) * sin

    def pad(x):
        if rows == RET_CHUNK:
            return x
        return jnp.concatenate([x, jnp.zeros((RET_CHUNK - rows, x.shape[1]), x.dtype)], axis=0)

    for h in range(RET_HEADS):
        hs = slice(h * RET_DIM, (h + 1) * RET_DIM)
        q = pad(rot(q_ref[:, hs]))
        k = pad(rot(k_ref[:, hs]) * (RET_DIM ** -0.5))
        v = pad(v_ref[:, hs])
        qb = q.astype(BF16)
        kb = k.astype(BF16)
        vb = v.astype(BF16)
        s = s_ref[0, h]
        att = lax.dot_general(qb, kb, NT_DIMS, preferred_element_type=F32) * dm_ref[h]
        o = jnp.dot(att.astype(BF16), vb, preferred_element_type=F32)
        o = o + jnp.dot(qb, s.astype(BF16), preferred_element_type=F32) * qd_ref[h]
        vdec = (v * kd_ref[h]).astype(BF16)
        s_ref[0, h] = s * cd_ref[h][0:1, :] + lax.dot_general(kb, vdec, TN_DIMS, preferred_element_type=F32)

        o = o[0:rows]
        mu = jnp.mean(o, axis=-1, keepdims=True)
        d = o - mu
        var = jnp.mean(d * d, axis=-1, keepdims=True)
        y = d * lax.rsqrt(var + EPS) * gg_ref[:, hs] + gb_ref[:, hs]
        g = g_ref[:, hs]
        o_ref[:, hs] = (g * jax.nn.sigmoid(g) * y).astype(BF16)


def _retention(z, s0, gn_gain, gn_bias, batch, t_pad, n_real, q_start):
    rows = min(t_pad, RET_CHUNK)
    nc = t_pad // rows
    assert n_real == t_pad or nc == 1, "padding rows are only supported in a single-chunk sequence"
    cos2, sin2 = _rotary_tables(q_start + jnp.arange(t_pad))
    dmat, qd, kd, cd = _decay_tables(min(n_real, rows))
    hb = RET_DIM
    rw = RET_WIDTH
    col = lambda base: (lambda b, c: (b * nc + c, base // rw))
    tab = lambda b, c: (0, 0, 0)
    o_r, s_new = pl.pallas_call(
        functools.partial(_retention_body, rows=rows),
        grid=(batch, nc),
        in_specs=[
            pl.BlockSpec((rows, rw), col(COL_RQ)),
            pl.BlockSpec((rows, rw), col(COL_RK)),
            pl.BlockSpec((rows, rw), col(COL_RV)),
            pl.BlockSpec((rows, rw), col(COL_RG)),
            pl.BlockSpec((rows, hb), lambda b, c: (c, 0)),
            pl.BlockSpec((rows, hb), lambda b, c: (c, 0)),
            pl.BlockSpec((RET_HEADS, RET_CHUNK, RET_CHUNK), tab),
            pl.BlockSpec((RET_HEADS, RET_CHUNK, LANES), tab),
            pl.BlockSpec((RET_HEADS, RET_CHUNK, LANES), tab),
            pl.BlockSpec((RET_HEADS, SUBLANES, LANES), tab),
            pl.BlockSpec((1, rw), lambda b, c: (0, 0)),
            pl.BlockSpec((1, rw), lambda b, c: (0, 0)),
            pl.BlockSpec((1, RET_HEADS, hb, hb), lambda b, c: (b, 0, 0, 0)),
        ],
        out_specs=[
            pl.BlockSpec((rows, rw), lambda b, c: (b * nc + c, 0)),
            pl.BlockSpec((1, RET_HEADS, hb, hb), lambda b, c: (b, 0, 0, 0)),
        ],
        out_shape=[
            jax.ShapeDtypeStruct((batch * t_pad, rw), BF16),
            jax.ShapeDtypeStruct((batch, RET_HEADS, hb, hb), F32),
        ],
        compiler_params=_params("arbitrary", "arbitrary"),
        name="retention",
    )(z, z, z, z, jnp.asarray(cos2), jnp.asarray(sin2), jnp.asarray(dmat), jnp.asarray(qd), jnp.asarray(kd),
      jnp.asarray(cd), gn_gain.reshape(1, rw), gn_bias.reshape(1, rw), s0)
    return o_r, s_new


def _conv_body(a_ref, g_ref, buf_ref, w_ref, b_ref, lg_ref, lb_ref, o_ref, nb_ref, ub_ref, uf_ref, *, tt, n_last):
    t = pl.program_id(1)

    def rounded(x):
        return x.astype(BF16).astype(F32)

    @pl.when(t == 0)
    def _():
        uf_ref[0:CONV_HALO, :] = buf_ref[0]
        ub_ref[0:CONV_HALO, :] = rounded(buf_ref[0])

    g = g_ref[...]
    u = a_ref[...] * jax.nn.sigmoid(g)
    uf_ref[CONV_HALO:CONV_HALO + tt, :] = u
    ub_ref[CONV_HALO:CONV_HALO + tt, :] = rounded(u)

    first = CONV_HALO - (CONV_K - 1)
    acc = jnp.broadcast_to(b_ref[...], (tt, CONV_WIDTH))
    for k in range(CONV_K):
        acc = acc + w_ref[k:k + 1, :] * ub_ref[first + k:first + k + tt, :]

    mu = jnp.mean(acc, axis=-1, keepdims=True)
    d = acc - mu
    var = jnp.mean(d * d, axis=-1, keepdims=True)
    y = d * lax.rsqrt(var + EPS) * lg_ref[...] + lb_ref[...]
    o_ref[...] = (y * jax.nn.sigmoid(y)).astype(BF16)

    @pl.when(t == pl.num_programs(1) - 1)
    def _():
        nb_ref[0] = uf_ref[n_last:n_last + CONV_HALO, :]

    ub_ref[0:CONV_HALO, :] = ub_ref[tt:tt + CONV_HALO, :]
    uf_ref[0:CONV_HALO, :] = uf_ref[tt:tt + CONV_HALO, :]


def _conv(z, buf, w, b, ln_g, ln_b, batch, t_pad, n_real):
    tt = min(t_pad, 512)
    nt = t_pad // tt
    n_last = n_real - (nt - 1) * tt
    first = CONV_HALO - (CONV_K - 1)
    buf_p = jnp.pad(buf, ((0, 0), (first, 0), (0, 0)))
    w_p = jnp.pad(w, ((0, CONV_HALO - CONV_K), (0, 0)))
    cw = CONV_WIDTH
    vec = lambda v: v.reshape(1, cw)
    o_c, nb = pl.pallas_call(
        functools.partial(_conv_body, tt=tt, n_last=n_last),
        grid=(batch, nt),
        in_specs=[
            pl.BlockSpec((tt, cw), lambda bb, t: (bb * nt + t, COL_CA // cw)),
            pl.BlockSpec((tt, cw), lambda bb, t: (bb * nt + t, COL_CG // cw)),
            pl.BlockSpec((1, CONV_HALO, cw), lambda bb, t: (bb, 0, 0)),
            pl.BlockSpec((CONV_HALO, cw), lambda bb, t: (0, 0)),
            pl.BlockSpec((1, cw), lambda bb, t: (0, 0)),
            pl.BlockSpec((1, cw), lambda bb, t: (0, 0)),
            pl.BlockSpec((1, cw), lambda bb, t: (0, 0)),
        ],
        out_specs=[
            pl.BlockSpec((tt, cw), lambda bb, t: (bb * nt + t, 0)),
            pl.BlockSpec((1, CONV_HALO, cw), lambda bb, t: (bb, 0, 0)),
        ],
        out_shape=[
            jax.ShapeDtypeStruct((batch * t_pad, cw), BF16),
            jax.ShapeDtypeStruct((batch, CONV_HALO, cw), F32),
        ],
        scratch_shapes=[pltpu.VMEM((CONV_HALO + tt, cw), F32), pltpu.VMEM((CONV_HALO + tt, cw), F32)],
        compiler_params=_params("arbitrary", "arbitrary"),
        name="conv",
    )(z, z, buf_p, w_p, vec(b), vec(ln_g), vec(ln_b))
    return o_c, nb[:, first:, :]


def _suffix_matrix(bk):
    j = np.arange(bk)[:, None]
    s = np.arange(bk)[None, :]
    m = np.concatenate([(j > s).astype(np.float32), np.ones((bk, LANES), np.float32)], axis=1)
    return jnp.asarray(m, dtype=BF16)


def _sb_tile(q, kblk, vblk, u, bias, c, acc, masked):
    bq = q.shape[0]
    bk = kblk.shape[0]
    s = lax.dot_general(q, kblk, NT_DIMS, preferred_element_type=F32)
    z = s * (SB_DIM ** -0.5) + bias
    sp = jnp.maximum(z, 0.0) + jnp.log(1.0 + jnp.exp(-jnp.abs(z)))
    neg = -sp
    if masked:
        row = lax.broadcasted_iota(I32, (bq, bk), 0)
        colk = lax.broadcasted_iota(I32, (bq, bk), 1)
        causal = colk < row
        neg = jnp.where(causal, neg, 0.0)
    rt = jnp.dot(neg.astype(BF16), u, preferred_element_type=F32)
    suffix = rt[:, :bk]
    total = rt[:, bk:]
    c_all = c if bk == LANES else jnp.concatenate([c] * (bk // LANES), axis=1)
    a = jnp.exp((z - sp) + c_all + suffix)
    if masked:
        a = jnp.where(causal, a, 0.0)
    acc = acc + jnp.dot(a.astype(BF16), vblk, preferred_element_type=F32)
    return c + total, acc


def _sb_prompt_body(bias_ref, q_ref, k_ref, v_ref, u_ref, o_ref, kb_ref, vb_ref, *, bq, hp):
    hg = pl.program_id(1)
    i = pl.program_id(2)

    @pl.when(i == 0)
    def _():
        kb_ref[...] = k_ref[...].astype(BF16)
        vb_ref[...] = v_ref[...].astype(BF16)

    u = u_ref[...]
    cols = [slice(j * SB_DIM, (j + 1) * SB_DIM) for j in range(hp)]
    qs = [q_ref[:, cs].astype(BF16) for cs in cols]
    biases = [bias_ref[hg * hp + j] for j in range(hp)]

    def tiles(st, carry, masked):
        out = []
        for j in range(hp):
            out += _sb_tile(qs[j], kb_ref[pl.ds(st, bq), cols[j]], vb_ref[pl.ds(st, bq), cols[j]], u, biases[j],
                            carry[2 * j], carry[2 * j + 1], masked)
        return tuple(out)

    zero = jnp.zeros((bq, LANES), F32)
    carry = tiles(pl.multiple_of(i * bq, bq), (zero,) * (2 * hp), True)
    carry = lax.fori_loop(0, i, lambda n, cr: tiles(pl.multiple_of((i - 1 - n) * bq, bq), cr, False), carry)
    for j in range(hp):
        o_ref[:, cols[j]] = carry[2 * j + 1].astype(BF16)


def _sb_prompt(z, sb_bias, batch, t, bq=256, hp=4):
    nq = t // bq
    hw = hp * SB_DIM
    return pl.pallas_call(
        functools.partial(_sb_prompt_body, bq=bq, hp=hp),
        grid=(batch, SB_HEADS // hp, nq),
        in_specs=[
            pl.BlockSpec(memory_space=pltpu.SMEM),
            pl.BlockSpec((bq, hw), lambda b, h, i: (b * nq + i, COL_SQ // hw + h)),
            pl.BlockSpec((t, hw), lambda b, h, i: (b, COL_SK // hw + h)),
            pl.BlockSpec((t, hw), lambda b, h, i: (b, COL_SV // hw + h)),
            pl.BlockSpec((bq, bq + LANES), lambda b, h, i: (0, 0)),
        ],
        out_specs=pl.BlockSpec((bq, hw), lambda b, h, i: (b * nq + i, h)),
        out_shape=jax.ShapeDtypeStruct((batch * t, SB_WIDTH), BF16),
        scratch_shapes=[pltpu.VMEM((t, hw), BF16), pltpu.VMEM((t, hw), BF16)],
        compiler_params=_params("arbitrary", "arbitrary", "arbitrary"),
        name="sb_prompt",
    )(sb_bias, z, z, z, _suffix_matrix(bq))


SB_PAGES_PER_STEP = 4


def _sb_sample_body(pt_ref, bias_ref, q_ref, kn_ref, vn_ref, ck_hbm, cv_hbm, u_ref, o_ref,
                    kbuf, vbuf, sem, c_ref, acc_ref, *, layer, n_pages, pps):
    b = pl.program_id(0)
    s = pl.program_id(1)
    ns = pl.num_programs(1)
    step = b * ns + s
    slot = step % 2
    tp = SAMPLE_T_PAD

    def start_fetch(bb, ss, sl):
        for j in range(pps):
            page = pt_ref[bb, n_pages - 1 - (ss * pps + j)]
            for h in range(SB_HEADS):
                pltpu.make_async_copy(ck_hbm.at[layer, page, :, h, :], kbuf.at[sl, j, h], sem.at[0, sl]).start()
                pltpu.make_async_copy(cv_hbm.at[layer, page, :, h, :], vbuf.at[sl, j, h], sem.at[1, sl]).start()

    def wait_fetch(sl):
        pltpu.make_async_copy(kbuf.at[sl], kbuf.at[sl], sem.at[0, sl]).wait()
        pltpu.make_async_copy(vbuf.at[sl], vbuf.at[sl], sem.at[1, sl]).wait()

    @pl.when(step == 0)
    def _():
        start_fetch(0, 0, 0)

    @pl.when(step + 1 < pl.num_programs(0) * ns)
    def _():
        last = s == ns - 1
        start_fetch(jnp.where(last, b + 1, b), jnp.where(last, 0, s + 1), 1 - slot)

    u = u_ref[...]
    qs = [q_ref[:, h * SB_DIM:(h + 1) * SB_DIM].astype(BF16) for h in range(SB_HEADS)]
    bias = jnp.concatenate([jnp.full((tp, LANES), bias_ref[h], F32) for h in range(SB_HEADS)], axis=0)

    def attend(head_k, head_v, c, acc, masked):
        sc = jnp.concatenate([lax.dot_general(qs[h], head_k(h).astype(BF16), NT_DIMS, preferred_element_type=F32)
                              for h in range(SB_HEADS)], axis=0)
        z = sc * (SB_DIM ** -0.5) + bias
        sp = jnp.maximum(z, 0.0) + jnp.log(1.0 + jnp.exp(-jnp.abs(z)))
        neg = -sp
        if masked:
            t = lax.broadcasted_iota(I32, z.shape, 0) & (tp - 1)
            causal = lax.broadcasted_iota(I32, z.shape, 1) < t
            neg = jnp.where(causal, neg, 0.0)
        rt = jnp.dot(neg.astype(BF16), u, preferred_element_type=F32)
        a = jnp.exp((z - sp) + c + rt[:, :PAGE_SIZE])
        if masked:
            a = jnp.where(causal, a, 0.0)
        ab = a.astype(BF16)
        pv = jnp.concatenate([jnp.dot(ab[h * tp:(h + 1) * tp], head_v(h).astype(BF16), preferred_element_type=F32)
                              for h in range(SB_HEADS)], axis=0)
        return c + rt[:, PAGE_SIZE:], acc + pv

    @pl.when(s == 0)
    def _():
        zero = jnp.zeros((SB_HEADS * tp, LANES), F32)
        c, acc = attend(lambda h: kn_ref[0, :, h * SB_DIM:(h + 1) * SB_DIM],
                        lambda h: vn_ref[0, :, h * SB_DIM:(h + 1) * SB_DIM], zero, zero, True)
        c_ref[...] = c
        acc_ref[...] = acc

    wait_fetch(slot)
    c = c_ref[...]
    acc = acc_ref[...]
    for j in range(pps):
        c, acc = attend(lambda h: kbuf[slot, j, h], lambda h: vbuf[slot, j, h], c, acc, False)
    c_ref[...] = c
    acc_ref[...] = acc

    @pl.when(s == ns - 1)
    def _():
        for h in range(SB_HEADS):
            o_ref[:, h * SB_DIM:(h + 1) * SB_DIM] = acc[h * tp:(h + 1) * tp].astype(BF16)


def _sb_sample(z, cache_k, cache_v, layer, page_table, sb_bias, batch):
    n_pages = page_table.shape[1]
    pps = SB_PAGES_PER_STEP
    assert n_pages % pps == 0
    tp = SAMPLE_T_PAD
    w = SB_WIDTH

    def new_page(col):
        rows = z[:, col:col + w].reshape(batch, tp, w)
        return jnp.pad(rows, ((0, 0), (0, PAGE_SIZE - tp), (0, 0)))

    page_buf = pltpu.VMEM((2, pps, SB_HEADS, PAGE_SIZE, SB_DIM), F32)
    grid_spec = pltpu.PrefetchScalarGridSpec(
        num_scalar_prefetch=1,
        grid=(batch, n_pages // pps),
        in_specs=[
            pl.BlockSpec(memory_space=pltpu.SMEM),
            pl.BlockSpec((tp, w), lambda b, s, pt: (b, COL_SQ // w)),
            pl.BlockSpec((1, PAGE_SIZE, w), lambda b, s, pt: (b, 0, 0)),
            pl.BlockSpec((1, PAGE_SIZE, w), lambda b, s, pt: (b, 0, 0)),
            pl.BlockSpec(memory_space=pl.ANY),
            pl.BlockSpec(memory_space=pl.ANY),
            pl.BlockSpec((PAGE_SIZE, PAGE_SIZE + LANES), lambda b, s, pt: (0, 0)),
        ],
        out_specs=pl.BlockSpec((tp, w), lambda b, s, pt: (b, 0)),
        scratch_shapes=[page_buf, page_buf, pltpu.SemaphoreType.DMA((2, 2)),
                        pltpu.VMEM((SB_HEADS * tp, LANES), F32), pltpu.VMEM((SB_HEADS * tp, LANES), F32)],
    )
    return pl.pallas_call(
        functools.partial(_sb_sample_body, layer=layer, n_pages=n_pages, pps=pps),
        grid_spec=grid_spec,
        out_shape=jax.ShapeDtypeStruct((batch * tp, w), BF16),
        compiler_params=_params("arbitrary", "arbitrary"),
        name="sb_sample",
    )(page_table, sb_bias, z, new_page(COL_SK), new_page(COL_SV), cache_k, cache_v, _suffix_matrix(PAGE_SIZE))


ROUTE_ROWS = SUBLANES
ROUTER_PAD = LANES


def _route_rows(lt):
    lg = [lt[g:g + 1, :] for g in range(N_GROUPS)]
    m = functools.reduce(jnp.maximum, lg)
    gi = jnp.full(m.shape, N_GROUPS - 1, I32)
    for g in range(N_GROUPS - 2, -1, -1):
        gi = jnp.where(lg[g] == m, g, gi)
    den = functools.reduce(jnp.add, [jnp.exp(x - m) for x in lg])
    p_group = 1.0 / den

    def expert_logit(k):
        out = lt[N_GROUPS + (N_GROUPS - 1) * EXPERTS_PER_GROUP + k:N_GROUPS + (N_GROUPS - 1) * EXPERTS_PER_GROUP + k + 1, :]
        for g in range(N_GROUPS - 2, -1, -1):
            r = N_GROUPS + g * EXPERTS_PER_GROUP + k
            out = jnp.where(gi == g, lt[r:r + 1, :], out)
        return out

    le = [expert_logit(k) for k in range(EXPERTS_PER_GROUP)]
    v1 = functools.reduce(jnp.maximum, le)
    i1 = jnp.full(m.shape, EXPERTS_PER_GROUP - 1, I32)
    for k in range(EXPERTS_PER_GROUP - 2, -1, -1):
        i1 = jnp.where(le[k] == v1, k, i1)
    rest = [jnp.where(i1 == k, -jnp.inf, le[k]) for k in range(EXPERTS_PER_GROUP)]
    v2 = functools.reduce(jnp.maximum, rest)
    i2 = jnp.full(m.shape, EXPERTS_PER_GROUP - 1, I32)
    for k in range(EXPERTS_PER_GROUP - 2, -1, -1):
        i2 = jnp.where(rest[k] == v2, k, i2)
    e2 = jnp.exp(v2 - v1)
    inv = 1.0 / (1.0 + e2)
    rows = [p_group * inv, p_group * (e2 * inv),
            (gi * EXPERTS_PER_GROUP + i1).astype(F32), (gi * EXPERTS_PER_GROUP + i2).astype(F32)]
    rows += [jnp.zeros_like(m)] * (ROUTE_ROWS - len(rows))
    return jnp.concatenate(rows, axis=0)


def _out_proj_body(x_ref, or_ref, oc_ref, os_ref, w_ref, g_ref, wr_ref, br_ref, x2_ref, h2_ref, rt_ref):
    y = jnp.dot(or_ref[...], w_ref[0:RET_WIDTH, :], preferred_element_type=F32)
    y = y + jnp.dot(oc_ref[...], w_ref[RET_WIDTH:RET_WIDTH + CONV_WIDTH, :], preferred_element_type=F32)
    y = y + jnp.dot(os_ref[...], w_ref[RET_WIDTH + CONV_WIDTH:, :], preferred_element_type=F32)
    x2 = x_ref[...] + y
    x2_ref[...] = x2
    ms = jnp.mean(x2 * x2, axis=-1, keepdims=True)
    h = x2 * lax.rsqrt(ms + EPS) * g_ref[...]
    h2_ref[...] = h
    lt = lax.dot_general(wr_ref[...], h.astype(BF16), NT_DIMS, preferred_element_type=F32)
    rt_ref[...] = _route_rows(lt + br_ref[...])


def _out_proj(x, o_r, o_c, o_s, w_out_bf16, norm_ffn, rg_w, rg_b, re_w, re_b, tm):
    n, d = x.shape
    wr = jnp.concatenate([rg_w, re_w], axis=1).T
    wr = jnp.pad(wr, ((0, ROUTER_PAD - wr.shape[0]), (0, 0))).astype(BF16)
    br = jnp.pad(jnp.concatenate([rg_b, re_b]), (0, ROUTER_PAD - N_GROUPS - N_EXPERTS)).reshape(ROUTER_PAD, 1)
    row = lambda width: pl.BlockSpec((tm, width), lambda i: (i, 0))
    full = lambda a, b: pl.BlockSpec((a, b), lambda i: (0, 0))
    return pl.pallas_call(
        _out_proj_body,
        grid=(n // tm,),
        in_specs=[row(d), row(RET_WIDTH), row(CONV_WIDTH), row(SB_WIDTH), full(d, d), full(1, d),
                  full(ROUTER_PAD, d), full(ROUTER_PAD, 1)],
        out_specs=[row(d), row(d), pl.BlockSpec((ROUTE_ROWS, tm), lambda i: (0, i))],
        out_shape=[jax.ShapeDtypeStruct((n, d), F32), jax.ShapeDtypeStruct((n, d), F32),
                   jax.ShapeDtypeStruct((ROUTE_ROWS, n), F32)],
        compiler_params=_params("arbitrary"),
        name="out_proj",
    )(x, o_r, o_c, o_s, w_out_bf16, norm_ffn.reshape(1, d), wr, br)


def _moe_plan(route, n_tok, bm):
    gates = route[0:TOP_K, :]
    experts = route[TOP_K:2 * TOP_K, :].astype(I32)
    a = TOP_K * n_tok
    n_blocks = (a + N_EXPERTS * (bm - 1) + bm - 1) // bm
    r = n_blocks * bm
    flat_e = experts.reshape(-1)
    onehot = (flat_e[:, None] == jnp.arange(N_EXPERTS, dtype=I32)[None, :]).astype(I32)
    rank = jnp.sum((jnp.cumsum(onehot, axis=0) - onehot) * onehot, axis=1)
    counts = jnp.sum(onehot, axis=0)
    padded = (counts + bm - 1) // bm * bm
    pad_ends = jnp.cumsum(padded)
    pad_starts = pad_ends - padded
    dest = pad_starts[flat_e] + rank
    slot_of_row = jnp.full((r,), -1, I32).at[dest].set(jnp.arange(a, dtype=I32))
    is_pad = slot_of_row < 0
    row_dst = jnp.maximum(slot_of_row, 0)
    row_tok = jnp.where(row_dst >= n_tok, row_dst - n_tok, row_dst)
    row_gate = jnp.where(is_pad, 0.0, gates.reshape(-1)[row_dst])
    n_used = (pad_ends[-1] // bm).astype(I32)
    blk = jnp.minimum(jnp.arange(n_blocks, dtype=I32), n_used - 1) * bm
    blk_expert = jnp.minimum(jnp.searchsorted(pad_ends, blk, side='right'), N_EXPERTS - 1).astype(I32)
    blk_valid = jnp.clip(pad_starts[blk_expert] + counts[blk_expert] - blk, 0, bm).astype(I32)
    gate_rows = jnp.broadcast_to(row_gate[:, None], (r, LANES))
    return row_tok, row_dst, blk_expert, blk_valid, n_used.reshape(1), gate_rows, n_blocks


MOE_DMA_UNROLL = 8


def _moe_body(tok_ref, dst_ref, be_ref, bv_ref, nu_ref, h_hbm, gate_ref, wg_ref, wu_ref, wd_ref, y_hbm,
              xbuf, ybuf, gsem, ssem, *, bm):
    i = pl.program_id(0)
    n_used = nu_ref[0]
    slot = i % 2
    unroll = min(MOE_DMA_UNROLL, bm)

    def start_gather(blk, s):
        def body(j, carry):
            tok = tok_ref[blk * bm + j]
            pltpu.make_async_copy(h_hbm.at[pl.ds(tok, 1), :], xbuf.at[s, pl.ds(j, 1), :], gsem.at[s]).start()
            return carry
        lax.fori_loop(0, bm, body, 0, unroll=unroll)

    def wait_gather(s):
        pltpu.make_async_copy(h_hbm.at[pl.ds(0, bm), :], xbuf.at[s], gsem.at[s]).wait()

    def start_scatter(blk, s):
        def row(j):
            dst = dst_ref[blk * bm + j]
            pltpu.make_async_copy(ybuf.at[s, pl.ds(j, 1), :], y_hbm.at[pl.ds(dst, 1), :], ssem.at[s]).start()

        nv = bv_ref[blk]

        def group(g, carry):
            for r in range(unroll):
                row(g * unroll + r)
            return carry

        def single(j, carry):
            row(j)
            return carry

        full = nv // unroll
        lax.fori_loop(0, full, group, 0)
        lax.fori_loop(full * unroll, nv, single, 0)

    def wait_scatter(blk, s):
        nv = bv_ref[blk]
        p = 1
        while p <= bm:
            @pl.when((nv & p) != 0)
            def _(p=p):
                pltpu.make_async_copy(ybuf.at[s, pl.ds(0, p), :], y_hbm.at[pl.ds(0, p), :], ssem.at[s]).wait()
            p *= 2

    @pl.when(i == 0)
    def _():
        start_gather(0, 0)

    @pl.when(i + 1 < n_used)
    def _():
        start_gather(i + 1, 1 - slot)

    @pl.when(i < n_used)
    def _():
        wait_gather(slot)

        @pl.when(i >= 2)
        def _():
            wait_scatter(i - 2, slot)

        x = xbuf[slot].astype(BF16)
        g = jnp.dot(x, wg_ref[0], preferred_element_type=F32)
        u = jnp.dot(x, wu_ref[0], preferred_element_type=F32)
        hm = (g * jax.nn.sigmoid(g) * u).astype(BF16)
        y = jnp.dot(hm, wd_ref[0], preferred_element_type=F32)
        gate = gate_ref[...].astype(BF16).astype(F32)
        ybuf[slot] = y.astype(BF16).astype(F32) * jnp.concatenate([gate] * (D_MODEL // LANES), axis=1)
        start_scatter(i, slot)

    @pl.when(i == pl.num_programs(0) - 1)
    def _():
        wait_scatter(n_used - 1, (n_used - 1) % 2)

        @pl.when(n_used >= 2)
        def _():
            wait_scatter(n_used - 2, n_used % 2)


def _moe(h2, route, wg_bf16, wu_bf16, wd_bf16, bm):
    n_tok, d = h2.shape
    row_tok, row_dst, blk_expert, blk_valid, n_used, gate_rows, n_blocks = _moe_plan(route, n_tok, bm)
    wspec = lambda a, b: pl.BlockSpec((1, a, b), lambda i, tok, dst, be, bv, nu: (be[i], 0, 0))
    grid_spec = pltpu.PrefetchScalarGridSpec(
        num_scalar_prefetch=5,
        grid=(n_blocks,),
        in_specs=[
            pl.BlockSpec(memory_space=pl.ANY),
            pl.BlockSpec((bm, LANES), lambda i, tok, dst, be, bv, nu: (i, 0)),
            wspec(d, D_EXPERT), wspec(d, D_EXPERT), wspec(D_EXPERT, d),
        ],
        out_specs=pl.BlockSpec(memory_space=pl.ANY),
        scratch_shapes=[pltpu.VMEM((2, bm, d), F32), pltpu.VMEM((2, bm, d), F32),
                        pltpu.SemaphoreType.DMA((2,)), pltpu.SemaphoreType.DMA((2,))],
    )
    return pl.pallas_call(
        functools.partial(_moe_body, bm=bm),
        grid_spec=grid_spec,
        out_shape=jax.ShapeDtypeStruct((TOP_K * n_tok, d), F32),
        compiler_params=_params("arbitrary"),
        name="moe",
    )(row_tok, row_dst, blk_expert, blk_valid, n_used, h2, gate_rows, wg_bf16, wu_bf16, wd_bf16)


def _combine_body(x_ref, y0_ref, y1_ref, g_ref, o_ref, *, final_norm):
    x = x_ref[...] + y0_ref[...] + y1_ref[...]
    if final_norm:
        ms = jnp.mean(x * x, axis=-1, keepdims=True)
        x = x * lax.rsqrt(ms + EPS) * g_ref[...]
    o_ref[...] = x


def _combine(x2, y, norm_gain, final_norm, tm):
    n, d = x2.shape
    nb = n // tm
    return pl.pallas_call(
        functools.partial(_combine_body, final_norm=final_norm),
        grid=(nb,),
        in_specs=[
            pl.BlockSpec((tm, d), lambda i: (i, 0)),
            pl.BlockSpec((tm, d), lambda i: (i, 0)),
            pl.BlockSpec((tm, d), lambda i: (i + nb, 0)),
            pl.BlockSpec((1, d), lambda i: (0, 0)),
        ],
        out_specs=pl.BlockSpec((tm, d), lambda i: (i, 0)),
        out_shape=jax.ShapeDtypeStruct((n, d), F32),
        compiler_params=_params("arbitrary"),
        name="combine",
    )(x2, y, y, norm_gain.reshape(1, d))


def _layer(x, l, last, group, w, state):
    batch, t_pad, n_real = group['batch'], group['t_pad'], group['n_real']
    z = _norm_proj(x, w['norm_mix'][l], w['w_in'][l], group['tm'])
    o_r, s_ret = _retention(z, state['ret'], w['ret_gn_gain'][l], w['ret_gn_bias'][l], batch, t_pad, n_real, group['q_start'])
    o_c, new_buf = _conv(z, state['conv'], w['conv_w'][l], w['conv_b'][l], w['conv_ln_gain'][l], w['conv_ln_bias'][l],
                         batch, t_pad, n_real)
    if group['paged']:
        o_s = _sb_sample(z, state['cache_k'], state['cache_v'], l, state['page_table'], w['sb_bias'][l], batch)
    else:
        o_s = _sb_prompt(z, w['sb_bias'][l], batch, t_pad)
    x2, h2, route = _out_proj(x, o_r, o_c, o_s, w['w_out'][l], w['norm_ffn'][l], w['router_group_w'][l],
                              w['router_group_b'][l], w['router_expert_w'][l], w['router_expert_b'][l], group['tm_out'])
    y = _moe(h2, route, w['w_gate'][l], w['w_up'][l], w['w_down'][l], group['bm'])
    x3 = _combine(x2, y, w['norm_final'], last, group['tm_out'])
    k_new = z[:, COL_SK:COL_SK + SB_WIDTH].reshape(batch, t_pad, SB_HEADS, SB_DIM)[:, :n_real]
    v_new = z[:, COL_SV:COL_SV + SB_WIDTH].reshape(batch, t_pad, SB_HEADS, SB_DIM)[:, :n_real]
    return x3, s_ret, new_buf, k_new, v_new


def kernel(x_prompt, x_sample, state_ret, state_conv, cache_k, cache_v, page_table, norm_mix, w_in, ret_gn_gain, ret_gn_bias, conv_w, conv_b, conv_ln_gain, conv_ln_bias, sb_bias, w_out, norm_ffn, router_group_w, router_group_b, router_expert_w, router_expert_b, w_gate, w_up, w_down, norm_final):
    b, t, d = x_prompt.shape
    db, dt, _ = x_sample.shape
    depth = w_in.shape[0]
    n_past = page_table.shape[1] * PAGE_SIZE
    w = dict(norm_mix=norm_mix, w_in=w_in.astype(BF16), ret_gn_gain=ret_gn_gain, ret_gn_bias=ret_gn_bias,
             conv_w=conv_w, conv_b=conv_b, conv_ln_gain=conv_ln_gain, conv_ln_bias=conv_ln_bias, sb_bias=sb_bias,
             w_out=w_out.astype(BF16), norm_ffn=norm_ffn, router_group_w=router_group_w, router_group_b=router_group_b,
             router_expert_w=router_expert_w, router_expert_b=router_expert_b,
             w_gate=w_gate.astype(BF16), w_up=w_up.astype(BF16), w_down=w_down.astype(BF16), norm_final=norm_final)
    prompt = dict(batch=b, t_pad=t, n_real=t, q_start=0, tm=1024, tm_out=256, bm=256, paged=False)
    sample = dict(batch=db, t_pad=SAMPLE_T_PAD, n_real=dt, q_start=n_past, tm=db * SAMPLE_T_PAD,
                  tm_out=db * SAMPLE_T_PAD, bm=SUBLANES, paged=True)

    xp = x_prompt.reshape(b * t, d)
    xs = jnp.pad(x_sample, ((0, 0), (0, SAMPLE_T_PAD - dt), (0, 0))).reshape(db * SAMPLE_T_PAD, d)
    outs_p, outs_s = [], []
    for l in range(depth):
        last = l == depth - 1
        st_p = dict(ret=jnp.zeros((b, RET_HEADS, RET_DIM, RET_DIM), F32), conv=jnp.zeros((b, CONV_K - 1, CONV_WIDTH), F32))
        xp, *lp = _layer(xp, l, last, prompt, w, st_p)
        outs_p.append(lp)
        st_s = dict(ret=state_ret[l], conv=state_conv[l], cache_k=cache_k, cache_v=cache_v, page_table=page_table)
        xs, *ls = _layer(xs, l, last, sample, w, st_s)
        outs_s.append(ls)
    y_prompt = xp.reshape(b, t, d)
    y_sample = xs.reshape(db, SAMPLE_T_PAD, d)[:, :dt]
    stack = lambda outs, j: jnp.stack([o[j] for o in outs])
    return (y_prompt, y_sample,
            stack(outs_p, 0), stack(outs_p, 1), stack(outs_p, 2), stack(outs_p, 3),
            stack(outs_s, 0), stack(outs_s, 1), stack(outs_s, 2), stack(outs_s, 3))
```

```python
import functools

import numpy as np
import jax
import jax.numpy as jnp
from jax import lax
from jax.experimental import pallas as pl
from jax.experimental.pallas import tpu as pltpu

F32 = jnp.float32
BF16 = jnp.bfloat16
I32 = jnp.int32

D_MODEL = 2048
RET_HEADS = 4
RET_DIM = 128
RET_WIDTH = RET_HEADS * RET_DIM
RET_CHUNK = 128
ROPE_BASE = 10000.0
CONV_WIDTH = 512
CONV_K = 31
SB_HEADS = 8
SB_DIM = 128
SB_WIDTH = SB_HEADS * SB_DIM
IN_WIDTH = 4 * RET_WIDTH + 2 * CONV_WIDTH + 3 * SB_WIDTH
N_GROUPS = 4
EXPERTS_PER_GROUP = 8
N_EXPERTS = N_GROUPS * EXPERTS_PER_GROUP
TOP_K = 2
D_EXPERT = 512
PAGE_SIZE = 128
EPS = 1e-6

COL_RQ, COL_RK, COL_RV, COL_RG = 0, 512, 1024, 1536
COL_CA, COL_CG = 2048, 2560
COL_SQ, COL_SK, COL_SV = 3072, 4096, 5120

LANES = 128
SUBLANES = 8
SAMPLE_T_PAD = SUBLANES
CONV_HALO = 32
VMEM_LIMIT = 56 * 1024 * 1024

NT_DIMS = (((1,), (1,)), ((), ()))
TN_DIMS = (((0,), (0,)), ((), ()))


def _params(*sem):
    return pltpu.CompilerParams(dimension_semantics=sem, vmem_limit_bytes=VMEM_LIMIT)


def _norm_proj_body(x_ref, g_ref, w_ref, z_ref, *rest, tm, tn, seq_len):
    i = pl.program_id(0)
    j = pl.program_id(1)
    if seq_len is None:
        (h_ref,) = rest
    else:
        k_hbm, v_hbm, h_ref, stage, sem = rest

    @pl.when(j == 0)
    def _():
        x = x_ref[...]
        ms = jnp.mean(x * x, axis=-1, keepdims=True)
        h_ref[...] = (x * lax.rsqrt(ms + EPS) * g_ref[...]).astype(BF16)

    z = jnp.dot(h_ref[...], w_ref[0], preferred_element_type=F32)
    z_ref[...] = z
    if seq_len is None:
        return

    heads_per_tile = tn // SB_DIM
    n_col_tiles = IN_WIDTH // tn
    first_kv = COL_SK // tn
    kv_tiles = [(first_kv + t, k_hbm, t * heads_per_tile) for t in range(SB_WIDTH // tn)]
    kv_tiles += [(COL_SV // tn + t, v_hbm, t * heads_per_tile) for t in range(SB_WIDTH // tn)]
    tiles_per_seq = seq_len // tm
    b = i // tiles_per_seq
    t0 = pl.multiple_of((i % tiles_per_seq) * tm, tm)

    def wait_tile():
        pltpu.make_async_copy(stage, stage, sem).wait()

    @pl.when((j > first_kv) | ((j == 0) & (i > 0)))
    def _():
        wait_tile()

    for jt, dst, head0 in kv_tiles:
        @pl.when(j == jt)
        def _(dst=dst, head0=head0):
            stage[...] = z
            for hh in range(heads_per_tile):
                pltpu.make_async_copy(stage.at[:, pl.ds(hh * SB_DIM, SB_DIM)],
                                      dst.at[b, pl.ds(t0, tm), head0 + hh, :], sem).start()

    @pl.when((i == pl.num_programs(0) - 1) & (j == n_col_tiles - 1))
    def _():
        wait_tile()


def _norm_proj(x, gain, w_bf16, layer, tm, tn=512, seq_len=None):
    n, d = x.shape
    dout = w_bf16.shape[2]
    kv = seq_len is not None
    if kv:
        assert COL_SV + SB_WIDTH == dout and COL_SV == COL_SK + SB_WIDTH and SB_WIDTH % tn == 0 and seq_len % tm == 0
    kv_shape = jax.ShapeDtypeStruct((n // seq_len, seq_len, SB_HEADS, SB_DIM), F32) if kv else None
    out = pl.pallas_call(
        functools.partial(_norm_proj_body, tm=tm, tn=tn, seq_len=seq_len),
        grid=(n // tm, dout // tn),
        in_specs=[
            pl.BlockSpec((tm, d), lambda i, j: (i, 0)),
            pl.BlockSpec((1, d), lambda i, j: (0, 0)),
            pl.BlockSpec((1, d, tn), lambda i, j: (layer, 0, j)),
        ],
        out_specs=[pl.BlockSpec((tm, tn), lambda i, j: (i, j))] + ([pl.BlockSpec(memory_space=pl.ANY)] * 2 if kv else []),
        out_shape=[jax.ShapeDtypeStruct((n, dout), F32)] + ([kv_shape, kv_shape] if kv else []),
        scratch_shapes=[pltpu.VMEM((tm, d), BF16)] + ([pltpu.VMEM((tm, tn), F32), pltpu.SemaphoreType.DMA(())] if kv else []),
        compiler_params=_params("arbitrary", "arbitrary"),
        name="norm_proj",
    )(x, gain.reshape(1, d), w_bf16)
    return out if kv else out[0]


def _rotary_tables(positions):
    half = RET_DIM // 2
    inv = ROPE_BASE ** (-jnp.arange(half, dtype=F32) / half)
    ang = positions.astype(F32)[:, None] * inv[None, :]
    cos = jnp.cos(ang)
    sin = jnp.sin(ang)
    return jnp.concatenate([cos, cos], axis=1), jnp.concatenate([-sin, sin], axis=1)


def _decay_tables(n_real):
    c = RET_CHUNK
    log_gamma = jnp.log(1.0 - 2.0 ** (-5.0 - jnp.arange(RET_HEADS, dtype=F32)))
    i = jnp.arange(c, dtype=F32)
    diff = i[:, None] - i[None, :]
    real = jnp.arange(c) < n_real
    dmat = jnp.where(diff >= 0, jnp.exp(log_gamma[:, None, None] * jnp.maximum(diff, 0.0)), 0.0)
    dmat = jnp.where((real[:, None] & real[None, :])[None], dmat, 0.0)
    q_decay = jnp.where(real[None, :], jnp.exp(log_gamma[:, None] * (i[None, :] + 1.0)), 0.0)
    k_decay = jnp.where(real[None, :], jnp.exp(log_gamma[:, None] * (n_real - 1.0 - i[None, :])), 0.0)
    c_decay = jnp.exp(log_gamma * n_real)
    rep = lambda t: jnp.broadcast_to(t[:, :, None], (RET_HEADS, c, LANES))
    cd = jnp.broadcast_to(c_decay[:, None, None], (RET_HEADS, SUBLANES, LANES))
    return dmat, rep(q_decay), rep(k_decay), cd


def _retention_body(q_ref, k_ref, v_ref, g_ref, cos_ref, sin_ref, dm_ref, qd_ref, kd_ref, cd_ref,
                    gg_ref, gb_ref, s0_ref, o_ref, s_ref, *, rows):
    @pl.when(pl.program_id(1) == 0)
    def _():
        s_ref[...] = s0_ref[...]

    cos = cos_ref[...]
    sin = sin_ref[...]

    def rot(x):
        return x * cos + pltpu.roll(x, RET_DIM // 2, 1) * sin

    def pad(x):
        if rows == RET_CHUNK:
            return x
        return jnp.concatenate([x, jnp.zeros((RET_CHUNK - rows, x.shape[1]), x.dtype)], axis=0)

    for h in range(RET_HEADS):
        hs = slice(h * RET_DIM, (h + 1) * RET_DIM)
        q = pad(rot(q_ref[:, hs]))
        k = pad(rot(k_ref[:, hs]) * (RET_DIM ** -0.5))
        v = pad(v_ref[:, hs])
        qb = q.astype(BF16)
        kb = k.astype(BF16)
        vb = v.astype(BF16)
        s = s_ref[0, h]
        att = lax.dot_general(qb, kb, NT_DIMS, preferred_element_type=F32) * dm_ref[h]
        o = jnp.dot(att.astype(BF16), vb, preferred_element_type=F32)
        o = o + jnp.dot(qb, s.astype(BF16), preferred_element_type=F32) * qd_ref[h]
        vdec = (v * kd_ref[h]).astype(BF16)
        s_ref[0, h] = s * cd_ref[h][0:1, :] + lax.dot_general(kb, vdec, TN_DIMS, preferred_element_type=F32)

        o = o[0:rows]
        mu = jnp.mean(o, axis=-1, keepdims=True)
        d = o - mu
        var = jnp.mean(d * d, axis=-1, keepdims=True)
        y = d * lax.rsqrt(var + EPS) * gg_ref[:, hs] + gb_ref[:, hs]
        g = g_ref[:, hs]
        o_ref[:, hs] = (g * jax.nn.sigmoid(g) * y).astype(BF16)


def _retention(z, s0, gn_gain, gn_bias, batch, t_pad, n_real, q_start):
    rows = min(t_pad, RET_CHUNK)
    nc = t_pad // rows
    assert n_real == t_pad or nc == 1, "padding rows are only supported in a single-chunk sequence"
    cos2, sin2 = _rotary_tables(q_start + jnp.arange(t_pad))
    dmat, qd, kd, cd = _decay_tables(min(n_real, rows))
    hb = RET_DIM
    rw = RET_WIDTH
    col = lambda base: (lambda b, c: (b * nc + c, base // rw))
    tab = lambda b, c: (0, 0, 0)
    o_r, s_new = pl.pallas_call(
        functools.partial(_retention_body, rows=rows),
        grid=(batch, nc),
        in_specs=[
            pl.BlockSpec((rows, rw), col(COL_RQ)),
            pl.BlockSpec((rows, rw), col(COL_RK)),
            pl.BlockSpec((rows, rw), col(COL_RV)),
            pl.BlockSpec((rows, rw), col(COL_RG)),
            pl.BlockSpec((rows, hb), lambda b, c: (c, 0)),
            pl.BlockSpec((rows, hb), lambda b, c: (c, 0)),
            pl.BlockSpec((RET_HEADS, RET_CHUNK, RET_CHUNK), tab),
            pl.BlockSpec((RET_HEADS, RET_CHUNK, LANES), tab),
            pl.BlockSpec((RET_HEADS, RET_CHUNK, LANES), tab),
            pl.BlockSpec((RET_HEADS, SUBLANES, LANES), tab),
            pl.BlockSpec((1, rw), lambda b, c: (0, 0)),
            pl.BlockSpec((1, rw), lambda b, c: (0, 0)),
            pl.BlockSpec((1, RET_HEADS, hb, hb), lambda b, c: (b, 0, 0, 0)),
        ],
        out_specs=[
            pl.BlockSpec((rows, rw), lambda b, c: (b * nc + c, 0)),
            pl.BlockSpec((1, RET_HEADS, hb, hb), lambda b, c: (b, 0, 0, 0)),
        ],
        out_shape=[
            jax.ShapeDtypeStruct((batch * t_pad, rw), BF16),
            jax.ShapeDtypeStruct((batch, RET_HEADS, hb, hb), F32),
        ],
        compiler_params=_params("arbitrary", "arbitrary"),
        name="retention",
    )(z, z, z, z, jnp.asarray(cos2), jnp.asarray(sin2), jnp.asarray(dmat), jnp.asarray(qd), jnp.asarray(kd),
      jnp.asarray(cd), gn_gain.reshape(1, rw), gn_bias.reshape(1, rw), s0)
    return o_r, s_new


def _conv_body(a_ref, g_ref, buf_ref, w_ref, b_ref, lg_ref, lb_ref, o_ref, nb_ref, ub_ref, uf_ref, *, tt, n_last):
    t = pl.program_id(1)

    def rounded(x):
        return x.astype(BF16).astype(F32)

    @pl.when(t == 0)
    def _():
        uf_ref[0:CONV_HALO, :] = buf_ref[0]
        ub_ref[0:CONV_HALO, :] = rounded(buf_ref[0])

    g = g_ref[...]
    u = a_ref[...] * jax.nn.sigmoid(g)
    uf_ref[CONV_HALO:CONV_HALO + tt, :] = u
    ub_ref[CONV_HALO:CONV_HALO + tt, :] = rounded(u)

    first = CONV_HALO - (CONV_K - 1)
    acc = jnp.broadcast_to(b_ref[...], (tt, CONV_WIDTH))
    for k in range(CONV_K):
        acc = acc + w_ref[k:k + 1, :] * ub_ref[first + k:first + k + tt, :]

    mu = jnp.mean(acc, axis=-1, keepdims=True)
    d = acc - mu
    var = jnp.mean(d * d, axis=-1, keepdims=True)
    y = d * lax.rsqrt(var + EPS) * lg_ref[...] + lb_ref[...]
    o_ref[...] = (y * jax.nn.sigmoid(y)).astype(BF16)

    @pl.when(t == pl.num_programs(1) - 1)
    def _():
        nb_ref[0] = uf_ref[n_last:n_last + CONV_HALO, :]

    ub_ref[0:CONV_HALO, :] = ub_ref[tt:tt + CONV_HALO, :]
    uf_ref[0:CONV_HALO, :] = uf_ref[tt:tt + CONV_HALO, :]


def _conv(z, buf, w, b, ln_g, ln_b, batch, t_pad, n_real):
    tt = min(t_pad, 512)
    nt = t_pad // tt
    n_last = n_real - (nt - 1) * tt
    first = CONV_HALO - (CONV_K - 1)
    buf_p = jnp.pad(buf, ((0, 0), (first, 0), (0, 0)))
    w_p = jnp.pad(w, ((0, CONV_HALO - CONV_K), (0, 0)))
    cw = CONV_WIDTH
    vec = lambda v: v.reshape(1, cw)
    o_c, nb = pl.pallas_call(
        functools.partial(_conv_body, tt=tt, n_last=n_last),
        grid=(batch, nt),
        in_specs=[
            pl.BlockSpec((tt, cw), lambda bb, t: (bb * nt + t, COL_CA // cw)),
            pl.BlockSpec((tt, cw), lambda bb, t: (bb * nt + t, COL_CG // cw)),
            pl.BlockSpec((1, CONV_HALO, cw), lambda bb, t: (bb, 0, 0)),
            pl.BlockSpec((CONV_HALO, cw), lambda bb, t: (0, 0)),
            pl.BlockSpec((1, cw), lambda bb, t: (0, 0)),
            pl.BlockSpec((1, cw), lambda bb, t: (0, 0)),
            pl.BlockSpec((1, cw), lambda bb, t: (0, 0)),
        ],
        out_specs=[
            pl.BlockSpec((tt, cw), lambda bb, t: (bb * nt + t, 0)),
            pl.BlockSpec((1, CONV_HALO, cw), lambda bb, t: (bb, 0, 0)),
        ],
        out_shape=[
            jax.ShapeDtypeStruct((batch * t_pad, cw), BF16),
            jax.ShapeDtypeStruct((batch, CONV_HALO, cw), F32),
        ],
        scratch_shapes=[pltpu.VMEM((CONV_HALO + tt, cw), F32), pltpu.VMEM((CONV_HALO + tt, cw), F32)],
        compiler_params=_params("arbitrary", "arbitrary"),
        name="conv",
    )(z, z, buf_p, w_p, vec(b), vec(ln_g), vec(ln_b))
    return o_c, nb[:, first:, :]


def _suffix_matrix(bk):
    j = np.arange(bk)[:, None]
    s = np.arange(bk)[None, :]
    m = np.concatenate([(j > s).astype(np.float32), np.ones((bk, LANES), np.float32)], axis=1)
    return jnp.asarray(m, dtype=BF16)


SB_STRIP = 256


def _sb_tile(q, kblk, vblk, u, bias, c, acc, masked):
    bq = q.shape[0]
    bk = kblk.shape[0]
    strip = min(SB_STRIP, bq)
    cs, accs = [], []
    for r0 in range(0, bq, strip):
        rows = slice(r0, r0 + strip)
        s = lax.dot_general(q[rows], kblk, NT_DIMS, preferred_element_type=F32)
        z = s * (SB_DIM ** -0.5) + bias
        sp = jnp.maximum(z, 0.0) + jnp.log(1.0 + jnp.exp(-jnp.abs(z)))
        if masked:
            row = lax.broadcasted_iota(I32, (strip, bk), 0) + r0
            causal = lax.broadcasted_iota(I32, (strip, bk), 1) < row
            spm = jnp.where(causal, sp, 0.0)
        else:
            spm = sp
        rt = jnp.dot(spm.astype(BF16), u, preferred_element_type=F32)
        cr = c[rows]
        c_all = cr if bk == LANES else jnp.concatenate([cr] * (bk // LANES), axis=1)
        a = jnp.exp((z - sp) - (c_all + rt[:, :bk]))
        if masked:
            a = jnp.where(causal, a, 0.0)
        accs.append(acc[rows] + jnp.dot(a.astype(BF16), vblk, preferred_element_type=F32))
        cs.append(cr + rt[:, bk:])
    if len(cs) == 1:
        return cs[0], accs[0]
    return jnp.concatenate(cs, axis=0), jnp.concatenate(accs, axis=0)


def _sb_prompt_body(bias_ref, q_ref, k_ref, v_ref, u_ref, o_ref, kb_ref, vb_ref, *, bq, hp):
    hg = pl.program_id(1)
    i = pl.program_id(2)

    @pl.when(i == 0)
    def _():
        kb_ref[...] = k_ref[...].astype(BF16)
        vb_ref[...] = v_ref[...].astype(BF16)

    u = u_ref[...]
    cols = [slice(j * SB_DIM, (j + 1) * SB_DIM) for j in range(hp)]
    qs = [q_ref[:, cs].astype(BF16) for cs in cols]
    biases = [bias_ref[hg * hp + j] for j in range(hp)]

    def tiles(st, carry, masked):
        out = []
        for j in range(hp):
            out += _sb_tile(qs[j], kb_ref[pl.ds(st, bq), cols[j]], vb_ref[pl.ds(st, bq), cols[j]], u, biases[j],
                            carry[2 * j], carry[2 * j + 1], masked)
        return tuple(out)

    zero = jnp.zeros((bq, LANES), F32)
    carry = tiles(pl.multiple_of(i * bq, bq), (zero,) * (2 * hp), True)
    carry = lax.fori_loop(0, i, lambda n, cr: tiles(pl.multiple_of((i - 1 - n) * bq, bq), cr, False), carry)
    for j in range(hp):
        o_ref[:, cols[j]] = carry[2 * j + 1].astype(BF16)


def _sb_prompt(z, sb_bias, batch, t, bq=256, hp=4):
    nq = t // bq
    hw = hp * SB_DIM
    return pl.pallas_call(
        functools.partial(_sb_prompt_body, bq=bq, hp=hp),
        grid=(batch, SB_HEADS // hp, nq),
        in_specs=[
            pl.BlockSpec(memory_space=pltpu.SMEM),
            pl.BlockSpec((bq, hw), lambda b, h, i: (b * nq + i, COL_SQ // hw + h)),
            pl.BlockSpec((t, hw), lambda b, h, i: (b, COL_SK // hw + h)),
            pl.BlockSpec((t, hw), lambda b, h, i: (b, COL_SV // hw + h)),
            pl.BlockSpec((bq, bq + LANES), lambda b, h, i: (0, 0)),
        ],
        out_specs=pl.BlockSpec((bq, hw), lambda b, h, i: (b * nq + i, h)),
        out_shape=jax.ShapeDtypeStruct((batch * t, SB_WIDTH), BF16),
        scratch_shapes=[pltpu.VMEM((t, hw), BF16), pltpu.VMEM((t, hw), BF16)],
        compiler_params=_params("arbitrary", "arbitrary", "arbitrary"),
        name="sb_prompt",
    )(sb_bias, z, z, z, _suffix_matrix(bq))


SB_PAGES_PER_STEP = 4


def _sb_sample_body(pt_ref, bias_ref, q_ref, kn_ref, vn_ref, ck_hbm, cv_hbm, u_ref, o_ref,
                    kbuf, vbuf, sem, c_ref, acc_ref, *, layer, n_pages, pps):
    b = pl.program_id(0)
    s = pl.program_id(1)
    ns = pl.num_programs(1)
    step = b * ns + s
    slot = step % 2
    tp = SAMPLE_T_PAD

    def start_fetch(bb, ss, sl):
        for j in range(pps):
            page = pt_ref[bb, n_pages - 1 - (ss * pps + j)]
            for h in range(SB_HEADS):
                pltpu.make_async_copy(ck_hbm.at[layer, page, :, h, :], kbuf.at[sl, j, h], sem.at[0, sl]).start()
                pltpu.make_async_copy(cv_hbm.at[layer, page, :, h, :], vbuf.at[sl, j, h], sem.at[1, sl]).start()

    def wait_fetch(sl):
        pltpu.make_async_copy(kbuf.at[sl], kbuf.at[sl], sem.at[0, sl]).wait()
        pltpu.make_async_copy(vbuf.at[sl], vbuf.at[sl], sem.at[1, sl]).wait()

    @pl.when(step == 0)
    def _():
        start_fetch(0, 0, 0)

    @pl.when(step + 1 < pl.num_programs(0) * ns)
    def _():
        last = s == ns - 1
        start_fetch(jnp.where(last, b + 1, b), jnp.where(last, 0, s + 1), 1 - slot)

    u = u_ref[...]
    qs = [q_ref[:, h * SB_DIM:(h + 1) * SB_DIM].astype(BF16) for h in range(SB_HEADS)]
    bias = jnp.concatenate([jnp.full((tp, LANES), bias_ref[h], F32) for h in range(SB_HEADS)], axis=0)

    def attend(head_k, head_v, c, acc, masked):
        sc = jnp.concatenate([lax.dot_general(qs[h], head_k(h).astype(BF16), NT_DIMS, preferred_element_type=F32)
                              for h in range(SB_HEADS)], axis=0)
        z = sc * (SB_DIM ** -0.5) + bias
        sp = jnp.maximum(z, 0.0) + jnp.log(1.0 + jnp.exp(-jnp.abs(z)))
        neg = -sp
        if masked:
            t = lax.broadcasted_iota(I32, z.shape, 0) & (tp - 1)
            causal = lax.broadcasted_iota(I32, z.shape, 1) < t
            neg = jnp.where(causal, neg, 0.0)
        rt = jnp.dot(neg.astype(BF16), u, preferred_element_type=F32)
        a = jnp.exp((z - sp) + c + rt[:, :PAGE_SIZE])
        if masked:
            a = jnp.where(causal, a, 0.0)
        ab = a.astype(BF16)
        pv = jnp.concatenate([jnp.dot(ab[h * tp:(h + 1) * tp], head_v(h).astype(BF16), preferred_element_type=F32)
                              for h in range(SB_HEADS)], axis=0)
        return c + rt[:, PAGE_SIZE:], acc + pv

    @pl.when(s == 0)
    def _():
        zero = jnp.zeros((SB_HEADS * tp, LANES), F32)
        c, acc = attend(lambda h: kn_ref[0, :, h * SB_DIM:(h + 1) * SB_DIM],
                        lambda h: vn_ref[0, :, h * SB_DIM:(h + 1) * SB_DIM], zero, zero, True)
        c_ref[...] = c
        acc_ref[...] = acc

    wait_fetch(slot)
    c = c_ref[...]
    acc = acc_ref[...]
    for j in range(pps):
        c, acc = attend(lambda h: kbuf[slot, j, h], lambda h: vbuf[slot, j, h], c, acc, False)
    c_ref[...] = c
    acc_ref[...] = acc

    @pl.when(s == ns - 1)
    def _():
        for h in range(SB_HEADS):
            o_ref[:, h * SB_DIM:(h + 1) * SB_DIM] = acc[h * tp:(h + 1) * tp].astype(BF16)


def _sb_sample(z, cache_k, cache_v, layer, page_table, sb_bias, batch):
    n_pages = page_table.shape[1]
    pps = SB_PAGES_PER_STEP
    assert n_pages % pps == 0
    tp = SAMPLE_T_PAD
    w = SB_WIDTH

    def new_page(col):
        rows = z[:, col:col + w].reshape(batch, tp, w)
        return jnp.pad(rows, ((0, 0), (0, PAGE_SIZE - tp), (0, 0)))

    page_buf = pltpu.VMEM((2, pps, SB_HEADS, PAGE_SIZE, SB_DIM), F32)
    grid_spec = pltpu.PrefetchScalarGridSpec(
        num_scalar_prefetch=1,
        grid=(batch, n_pages // pps),
        in_specs=[
            pl.BlockSpec(memory_space=pltpu.SMEM),
            pl.BlockSpec((tp, w), lambda b, s, pt: (b, COL_SQ // w)),
            pl.BlockSpec((1, PAGE_SIZE, w), lambda b, s, pt: (b, 0, 0)),
            pl.BlockSpec((1, PAGE_SIZE, w), lambda b, s, pt: (b, 0, 0)),
            pl.BlockSpec(memory_space=pl.ANY),
            pl.BlockSpec(memory_space=pl.ANY),
            pl.BlockSpec((PAGE_SIZE, PAGE_SIZE + LANES), lambda b, s, pt: (0, 0)),
        ],
        out_specs=pl.BlockSpec((tp, w), lambda b, s, pt: (b, 0)),
        scratch_shapes=[page_buf, page_buf, pltpu.SemaphoreType.DMA((2, 2)),
                        pltpu.VMEM((SB_HEADS * tp, LANES), F32), pltpu.VMEM((SB_HEADS * tp, LANES), F32)],
    )
    return pl.pallas_call(
        functools.partial(_sb_sample_body, layer=layer, n_pages=n_pages, pps=pps),
        grid_spec=grid_spec,
        out_shape=jax.ShapeDtypeStruct((batch * tp, w), BF16),
        compiler_params=_params("arbitrary", "arbitrary"),
        name="sb_sample",
    )(page_table, sb_bias, z, new_page(COL_SK), new_page(COL_SV), cache_k, cache_v, _suffix_matrix(PAGE_SIZE))


ROUTE_ROWS = SUBLANES
ROUTER_PAD = LANES


def _route_rows(lt):
    lg = [lt[g:g + 1, :] for g in range(N_GROUPS)]
    m = functools.reduce(jnp.maximum, lg)
    gi = jnp.full(m.shape, N_GROUPS - 1, I32)
    for g in range(N_GROUPS - 2, -1, -1):
        gi = jnp.where(lg[g] == m, g, gi)
    den = functools.reduce(jnp.add, [jnp.exp(x - m) for x in lg])
    p_group = 1.0 / den

    def expert_logit(k):
        out = lt[N_GROUPS + (N_GROUPS - 1) * EXPERTS_PER_GROUP + k:N_GROUPS + (N_GROUPS - 1) * EXPERTS_PER_GROUP + k + 1, :]
        for g in range(N_GROUPS - 2, -1, -1):
            r = N_GROUPS + g * EXPERTS_PER_GROUP + k
            out = jnp.where(gi == g, lt[r:r + 1, :], out)
        return out

    le = [expert_logit(k) for k in range(EXPERTS_PER_GROUP)]
    v1 = functools.reduce(jnp.maximum, le)
    i1 = jnp.full(m.shape, EXPERTS_PER_GROUP - 1, I32)
    for k in range(EXPERTS_PER_GROUP - 2, -1, -1):
        i1 = jnp.where(le[k] == v1, k, i1)
    rest = [jnp.where(i1 == k, -jnp.inf, le[k]) for k in range(EXPERTS_PER_GROUP)]
    v2 = functools.reduce(jnp.maximum, rest)
    i2 = jnp.full(m.shape, EXPERTS_PER_GROUP - 1, I32)
    for k in range(EXPERTS_PER_GROUP - 2, -1, -1):
        i2 = jnp.where(rest[k] == v2, k, i2)
    e2 = jnp.exp(v2 - v1)
    inv = 1.0 / (1.0 + e2)
    rows = [p_group * inv, p_group * (e2 * inv),
            (gi * EXPERTS_PER_GROUP + i1).astype(F32), (gi * EXPERTS_PER_GROUP + i2).astype(F32)]
    rows += [jnp.zeros_like(m)] * (ROUTE_ROWS - len(rows))
    return jnp.concatenate(rows, axis=0)


def _out_proj_body(x_ref, or_ref, oc_ref, os_ref, w_ref, g_ref, wr_ref, br_ref, x2_ref, h2_ref, rt_ref):
    y = jnp.dot(or_ref[...], w_ref[0:RET_WIDTH, :], preferred_element_type=F32)
    y = y + jnp.dot(oc_ref[...], w_ref[RET_WIDTH:RET_WIDTH + CONV_WIDTH, :], preferred_element_type=F32)
    y = y + jnp.dot(os_ref[...], w_ref[RET_WIDTH + CONV_WIDTH:, :], preferred_element_type=F32)
    x2 = x_ref[...] + y
    x2_ref[...] = x2
    ms = jnp.mean(x2 * x2, axis=-1, keepdims=True)
    h = x2 * lax.rsqrt(ms + EPS) * g_ref[...]
    h2_ref[...] = h
    lt = lax.dot_general(wr_ref[...], h.astype(BF16), NT_DIMS, preferred_element_type=F32)
    rt_ref[...] = _route_rows(lt + br_ref[...])


def _out_proj(x, o_r, o_c, o_s, w_out_bf16, norm_ffn, rg_w, rg_b, re_w, re_b, tm):
    n, d = x.shape
    wr = jnp.concatenate([rg_w, re_w], axis=1).T
    wr = jnp.pad(wr, ((0, ROUTER_PAD - wr.shape[0]), (0, 0))).astype(BF16)
    br = jnp.pad(jnp.concatenate([rg_b, re_b]), (0, ROUTER_PAD - N_GROUPS - N_EXPERTS)).reshape(ROUTER_PAD, 1)
    row = lambda width: pl.BlockSpec((tm, width), lambda i: (i, 0))
    full = lambda a, b: pl.BlockSpec((a, b), lambda i: (0, 0))
    return pl.pallas_call(
        _out_proj_body,
        grid=(n // tm,),
        in_specs=[row(d), row(RET_WIDTH), row(CONV_WIDTH), row(SB_WIDTH), full(d, d), full(1, d),
                  full(ROUTER_PAD, d), full(ROUTER_PAD, 1)],
        out_specs=[row(d), row(d), pl.BlockSpec((ROUTE_ROWS, tm), lambda i: (0, i))],
        out_shape=[jax.ShapeDtypeStruct((n, d), F32), jax.ShapeDtypeStruct((n, d), F32),
                   jax.ShapeDtypeStruct((ROUTE_ROWS, n), F32)],
        compiler_params=_params("arbitrary"),
        name="out_proj",
    )(x, o_r, o_c, o_s, w_out_bf16, norm_ffn.reshape(1, d), wr, br)


def _moe_plan(route, n_tok, bm):
    gates = route[0:TOP_K, :]
    experts = route[TOP_K:2 * TOP_K, :].astype(I32)
    a = TOP_K * n_tok
    n_blocks = (a + N_EXPERTS * (bm - 1) + bm - 1) // bm
    r = n_blocks * bm
    flat_e = experts.reshape(-1)
    onehot = (flat_e[:, None] == jnp.arange(N_EXPERTS, dtype=I32)[None, :]).astype(I32)
    rank = jnp.sum((jnp.cumsum(onehot, axis=0) - onehot) * onehot, axis=1)
    counts = jnp.sum(onehot, axis=0)
    padded = (counts + bm - 1) // bm * bm
    pad_ends = jnp.cumsum(padded)
    pad_starts = pad_ends - padded
    dest = pad_starts[flat_e] + rank
    slot_of_row = jnp.full((r,), -1, I32).at[dest].set(jnp.arange(a, dtype=I32))
    is_pad = slot_of_row < 0
    row_dst = jnp.maximum(slot_of_row, 0)
    row_tok = jnp.where(row_dst >= n_tok, row_dst - n_tok, row_dst)
    row_gate = jnp.where(is_pad, 0.0, gates.reshape(-1)[row_dst])
    n_used = (pad_ends[-1] // bm).astype(I32)
    blk = jnp.minimum(jnp.arange(n_blocks, dtype=I32), n_used - 1) * bm
    blk_expert = jnp.minimum(jnp.searchsorted(pad_ends, blk, side='right'), N_EXPERTS - 1).astype(I32)
    blk_valid = jnp.clip(pad_starts[blk_expert] + counts[blk_expert] - blk, 0, bm).astype(I32)
    gate_rows = jnp.broadcast_to(row_gate[:, None], (r, LANES))
    return row_tok, row_dst, blk_expert, blk_valid, n_used.reshape(1), gate_rows, n_blocks


def _moe_body(tok_ref, dst_ref, be_ref, bv_ref, nu_ref, h_hbm, gate_ref, wg_ref, wu_ref, wd_ref, y_hbm,
              xbuf, ybuf, wgb, wub, wdb, gsem, ssem, *, bm):
    i = pl.program_id(0)
    n_used = nu_ref[0]
    slot = i % 2

    def start_gather(blk, s):
        def group(g, carry):
            for r in range(SUBLANES):
                tok = tok_ref[blk * bm + g * SUBLANES + r]
                pltpu.make_async_copy(h_hbm.at[pl.ds(tok, 1), :], xbuf.at[s, g, pl.ds(r, 1), :], gsem.at[s]).start()
            return carry
        lax.fori_loop(0, bm // SUBLANES, group, 0)

    def wait_gather(s):
        pltpu.make_async_copy(xbuf.at[s], xbuf.at[s], gsem.at[s]).wait()

    def start_scatter(blk, s):
        def row(g, r):
            dst = dst_ref[blk * bm + g * SUBLANES + r]
            pltpu.make_async_copy(ybuf.at[s, g, pl.ds(r, 1), :], y_hbm.at[pl.ds(dst, 1), :], ssem.at[s]).start()

        nv = bv_ref[blk]

        def group(g, carry):
            for r in range(SUBLANES):
                row(g, r)
            return carry

        def single(j, carry):
            row(j // SUBLANES, j % SUBLANES)
            return carry

        full = nv // SUBLANES
        lax.fori_loop(0, full, group, 0)
        lax.fori_loop(full * SUBLANES, nv, single, 0)

    def wait_scatter(blk, s):
        nv = bv_ref[blk]
        p = 1
        while p <= bm:
            chunk = ybuf.at[s, 0, pl.ds(0, p), :] if p < SUBLANES else ybuf.at[s, pl.ds(0, p // SUBLANES)]

            @pl.when((nv & p) != 0)
            def _(chunk=chunk):
                pltpu.make_async_copy(chunk, chunk, ssem.at[s]).wait()
            p *= 2

    @pl.when(i == 0)
    def _():
        start_gather(0, 0)

    @pl.when(i + 1 < n_used)
    def _():
        start_gather(i + 1, 1 - slot)

    @pl.when(i < n_used)
    def _():
        wait_gather(slot)

        @pl.when(i >= 2)
        def _():
            wait_scatter(i - 2, slot)

        @pl.when((i == 0) | (be_ref[i] != be_ref[jnp.maximum(i - 1, 0)]))
        def _():
            wgb[...] = wg_ref[0, 0].astype(BF16)
            wub[...] = wu_ref[0, 0].astype(BF16)
            wdb[...] = wd_ref[0, 0].astype(BF16)

        x = xbuf[slot].reshape(bm, D_MODEL).astype(BF16)
        g = jnp.dot(x, wgb[...], preferred_element_type=F32)
        u = jnp.dot(x, wub[...], preferred_element_type=F32)
        hm = (g * jax.nn.sigmoid(g) * u).astype(BF16)
        y = jnp.dot(hm, wdb[...], preferred_element_type=F32)
        gate = gate_ref[...].astype(BF16).astype(F32)
        y = y.astype(BF16).astype(F32) * jnp.concatenate([gate] * (D_MODEL // LANES), axis=1)
        ybuf[slot] = y.reshape(bm // SUBLANES, SUBLANES, D_MODEL)
        start_scatter(i, slot)

    @pl.when(i == pl.num_programs(0) - 1)
    def _():
        wait_scatter(n_used - 1, (n_used - 1) % 2)

        @pl.when(n_used >= 2)
        def _():
            wait_scatter(n_used - 2, n_used % 2)


def _moe(h2, route, w_gate, w_up, w_down, layer, bm):
    n_tok, d = h2.shape
    row_tok, row_dst, blk_expert, blk_valid, n_used, gate_rows, n_blocks = _moe_plan(route, n_tok, bm)
    wspec = lambda a, b: pl.BlockSpec((1, 1, a, b), lambda i, tok, dst, be, bv, nu: (layer, be[i], 0, 0))
    grid_spec = pltpu.PrefetchScalarGridSpec(
        num_scalar_prefetch=5,
        grid=(n_blocks,),
        in_specs=[
            pl.BlockSpec(memory_space=pl.ANY),
            pl.BlockSpec((bm, LANES), lambda i, tok, dst, be, bv, nu: (i, 0)),
            wspec(d, D_EXPERT), wspec(d, D_EXPERT), wspec(D_EXPERT, d),
        ],
        out_specs=pl.BlockSpec(memory_space=pl.ANY),
        scratch_shapes=[pltpu.VMEM((2, bm // SUBLANES, SUBLANES, d), F32), pltpu.VMEM((2, bm // SUBLANES, SUBLANES, d), F32),
                        pltpu.VMEM((d, D_EXPERT), BF16), pltpu.VMEM((d, D_EXPERT), BF16), pltpu.VMEM((D_EXPERT, d), BF16),
                        pltpu.SemaphoreType.DMA((2,)), pltpu.SemaphoreType.DMA((2,))],
    )
    return pl.pallas_call(
        functools.partial(_moe_body, bm=bm),
        grid_spec=grid_spec,
        out_shape=jax.ShapeDtypeStruct((TOP_K * n_tok, d), F32),
        compiler_params=_params("arbitrary"),
        name="moe",
    )(row_tok, row_dst, blk_expert, blk_valid, n_used, h2, gate_rows, w_gate, w_up, w_down)


def _combine_body(x_ref, y0_ref, y1_ref, g_ref, o_ref, *, final_norm):
    x = x_ref[...] + y0_ref[...] + y1_ref[...]
    if final_norm:
        ms = jnp.mean(x * x, axis=-1, keepdims=True)
        x = x * lax.rsqrt(ms + EPS) * g_ref[...]
    o_ref[...] = x


def _combine(x2, y, norm_gain, final_norm, tm):
    n, d = x2.shape
    nb = n // tm
    return pl.pallas_call(
        functools.partial(_combine_body, final_norm=final_norm),
        grid=(nb,),
        in_specs=[
            pl.BlockSpec((tm, d), lambda i: (i, 0)),
            pl.BlockSpec((tm, d), lambda i: (i, 0)),
            pl.BlockSpec((tm, d), lambda i: (i + nb, 0)),
            pl.BlockSpec((1, d), lambda i: (0, 0)),
        ],
        out_specs=pl.BlockSpec((tm, d), lambda i: (i, 0)),
        out_shape=jax.ShapeDtypeStruct((n, d), F32),
        compiler_params=_params("arbitrary"),
        name="combine",
    )(x2, y, y, norm_gain.reshape(1, d))


def _layer(x, l, last, group, w, state):
    batch, t_pad, n_real = group['batch'], group['t_pad'], group['n_real']
    if group['paged']:
        z = _norm_proj(x, w['norm_mix'][l], w['w_in'], l, group['tm'])
        k_new = z[:, COL_SK:COL_SK + SB_WIDTH].reshape(batch, t_pad, SB_HEADS, SB_DIM)[:, :n_real]
        v_new = z[:, COL_SV:COL_SV + SB_WIDTH].reshape(batch, t_pad, SB_HEADS, SB_DIM)[:, :n_real]
    else:
        z, k_new, v_new = _norm_proj(x, w['norm_mix'][l], w['w_in'], l, group['tm'], seq_len=t_pad)
    o_r, s_ret = _retention(z, state['ret'], w['ret_gn_gain'][l], w['ret_gn_bias'][l], batch, t_pad, n_real, group['q_start'])
    o_c, new_buf = _conv(z, state['conv'], w['conv_w'][l], w['conv_b'][l], w['conv_ln_gain'][l], w['conv_ln_bias'][l],
                         batch, t_pad, n_real)
    if group['paged']:
        o_s = _sb_sample(z, state['cache_k'], state['cache_v'], l, state['page_table'], w['sb_bias'][l], batch)
    else:
        o_s = _sb_prompt(z, w['sb_bias'][l], batch, t_pad)
    x2, h2, route = _out_proj(x, o_r, o_c, o_s, w['w_out'][l], w['norm_ffn'][l], w['router_group_w'][l],
                              w['router_group_b'][l], w['router_expert_w'][l], w['router_expert_b'][l], group['tm_out'])
    y = _moe(h2, route, w['w_gate'], w['w_up'], w['w_down'], l, group['bm'])
    x3 = _combine(x2, y, w['norm_final'], last, group['tm_out'])
    return x3, s_ret, new_buf, k_new, v_new


def kernel(x_prompt, x_sample, state_ret, state_conv, cache_k, cache_v, page_table, norm_mix, w_in, ret_gn_gain, ret_gn_bias, conv_w, conv_b, conv_ln_gain, conv_ln_bias, sb_bias, w_out, norm_ffn, router_group_w, router_group_b, router_expert_w, router_expert_b, w_gate, w_up, w_down, norm_final):
    b, t, d = x_prompt.shape
    db, dt, _ = x_sample.shape
    depth = w_in.shape[0]
    n_past = page_table.shape[1] * PAGE_SIZE
    w = dict(norm_mix=norm_mix, w_in=w_in.astype(BF16), ret_gn_gain=ret_gn_gain, ret_gn_bias=ret_gn_bias,
             conv_w=conv_w, conv_b=conv_b, conv_ln_gain=conv_ln_gain, conv_ln_bias=conv_ln_bias, sb_bias=sb_bias,
             w_out=w_out.astype(BF16), norm_ffn=norm_ffn, router_group_w=router_group_w, router_group_b=router_group_b,
             router_expert_w=router_expert_w, router_expert_b=router_expert_b,
             w_gate=w_gate, w_up=w_up, w_down=w_down, norm_final=norm_final)
    prompt = dict(batch=b, t_pad=t, n_real=t, q_start=0, tm=1024, tm_out=256, bm=256, paged=False)
    sample = dict(batch=db, t_pad=SAMPLE_T_PAD, n_real=dt, q_start=n_past, tm=db * SAMPLE_T_PAD,
                  tm_out=db * SAMPLE_T_PAD, bm=SUBLANES, paged=True)

    xp = x_prompt.reshape(b * t, d)
    xs = jnp.pad(x_sample, ((0, 0), (0, SAMPLE_T_PAD - dt), (0, 0))).reshape(db * SAMPLE_T_PAD, d)
    outs_p, outs_s = [], []
    for l in range(depth):
        last = l == depth - 1
        st_p = dict(ret=jnp.zeros((b, RET_HEADS, RET_DIM, RET_DIM), F32), conv=jnp.zeros((b, CONV_K - 1, CONV_WIDTH), F32))
        xp, *lp = _layer(xp, l, last, prompt, w, st_p)
        outs_p.append(lp)
        st_s = dict(ret=state_ret[l], conv=state_conv[l], cache_k=cache_k, cache_v=cache_v, page_table=page_table)
        xs, *ls = _layer(xs, l, last, sample, w, st_s)
        outs_s.append(ls)
    y_prompt = xp.reshape(b, t, d)
    y_sample = xs.reshape(db, SAMPLE_T_PAD, d)[:, :dt]
    stack = lambda outs, j: jnp.stack([o[j] for o in outs])
    return (y_prompt, y_sample,
            stack(outs_p, 0), stack(outs_p, 1), stack(outs_p, 2), stack(outs_p, 3),
            stack(outs_s, 0), stack(outs_s, 1), stack(outs_s, 2), stack(outs_s, 3))
```

```python
import functools

import numpy as np
import jax
import jax.numpy as jnp
from jax import lax
from jax.experimental import pallas as pl
from jax.experimental.pallas import tpu as pltpu

F32 = jnp.float32
BF16 = jnp.bfloat16
I32 = jnp.int32

D_MODEL = 2048
RET_HEADS = 4
RET_DIM = 128
RET_WIDTH = RET_HEADS * RET_DIM
RET_CHUNK = 128
ROPE_BASE = 10000.0
CONV_WIDTH = 512
CONV_K = 31
SB_HEADS = 8
SB_DIM = 128
SB_WIDTH = SB_HEADS * SB_DIM
IN_WIDTH = 4 * RET_WIDTH + 2 * CONV_WIDTH + 3 * SB_WIDTH
N_GROUPS = 4
EXPERTS_PER_GROUP = 8
N_EXPERTS = N_GROUPS * EXPERTS_PER_GROUP
TOP_K = 2
D_EXPERT = 512
PAGE_SIZE = 128
EPS = 1e-6

COL_RQ, COL_RK, COL_RV, COL_RG = 0, 512, 1024, 1536
COL_CA, COL_CG = 2048, 2560
COL_SQ, COL_SK, COL_SV = 3072, 4096, 5120

LANES = 128
SUBLANES = 8
SAMPLE_T_PAD = SUBLANES
CONV_HALO = 32
VMEM_LIMIT = 56 * 1024 * 1024

NT_DIMS = (((1,), (1,)), ((), ()))
TN_DIMS = (((0,), (0,)), ((), ()))


def _params(*sem):
    return pltpu.CompilerParams(dimension_semantics=sem, vmem_limit_bytes=VMEM_LIMIT)


def _norm_proj_body(x_ref, g_ref, w_ref, z_ref, *rest, tm, tn, seq_len):
    i = pl.program_id(0)
    j = pl.program_id(1)
    if seq_len is None:
        (h_ref,) = rest
    else:
        k_hbm, v_hbm, h_ref, stage, sem = rest

    @pl.when(j == 0)
    def _():
        x = x_ref[...]
        ms = jnp.mean(x * x, axis=-1, keepdims=True)
        h_ref[...] = (x * lax.rsqrt(ms + EPS) * g_ref[...]).astype(BF16)

    z = jnp.dot(h_ref[...], w_ref[0], preferred_element_type=F32)
    z_ref[...] = z
    if seq_len is None:
        return

    heads_per_tile = tn // SB_DIM
    n_col_tiles = IN_WIDTH // tn
    first_kv = COL_SK // tn
    kv_tiles = [(first_kv + t, k_hbm, t * heads_per_tile) for t in range(SB_WIDTH // tn)]
    kv_tiles += [(COL_SV // tn + t, v_hbm, t * heads_per_tile) for t in range(SB_WIDTH // tn)]
    tiles_per_seq = seq_len // tm
    b = i // tiles_per_seq
    t0 = pl.multiple_of((i % tiles_per_seq) * tm, tm)

    def wait_tile():
        pltpu.make_async_copy(stage, stage, sem).wait()

    @pl.when((j > first_kv) | ((j == 0) & (i > 0)))
    def _():
        wait_tile()

    for jt, dst, head0 in kv_tiles:
        @pl.when(j == jt)
        def _(dst=dst, head0=head0):
            stage[...] = z
            for hh in range(heads_per_tile):
                pltpu.make_async_copy(stage.at[:, pl.ds(hh * SB_DIM, SB_DIM)],
                                      dst.at[b, pl.ds(t0, tm), head0 + hh, :], sem).start()

    @pl.when((i == pl.num_programs(0) - 1) & (j == n_col_tiles - 1))
    def _():
        wait_tile()


def _norm_proj(x, gain, w_bf16, layer, tm, tn=512, seq_len=None):
    n, d = x.shape
    dout = w_bf16.shape[2]
    kv = seq_len is not None
    if kv:
        assert COL_SV + SB_WIDTH == dout and COL_SV == COL_SK + SB_WIDTH and SB_WIDTH % tn == 0 and seq_len % tm == 0
    kv_shape = jax.ShapeDtypeStruct((n // seq_len, seq_len, SB_HEADS, SB_DIM), F32) if kv else None
    out = pl.pallas_call(
        functools.partial(_norm_proj_body, tm=tm, tn=tn, seq_len=seq_len),
        grid=(n // tm, dout // tn),
        in_specs=[
            pl.BlockSpec((tm, d), lambda i, j: (i, 0)),
            pl.BlockSpec((1, d), lambda i, j: (0, 0)),
            pl.BlockSpec((1, d, tn), lambda i, j: (layer, 0, j)),
        ],
        out_specs=[pl.BlockSpec((tm, tn), lambda i, j: (i, j))] + ([pl.BlockSpec(memory_space=pl.ANY)] * 2 if kv else []),
        out_shape=[jax.ShapeDtypeStruct((n, dout), F32)] + ([kv_shape, kv_shape] if kv else []),
        scratch_shapes=[pltpu.VMEM((tm, d), BF16)] + ([pltpu.VMEM((tm, tn), F32), pltpu.SemaphoreType.DMA(())] if kv else []),
        compiler_params=_params("arbitrary", "arbitrary"),
        name="norm_proj",
    )(x, gain.reshape(1, d), w_bf16)
    return out if kv else out[0]


def _rotary_tables(positions):
    half = RET_DIM // 2
    inv = ROPE_BASE ** (-jnp.arange(half, dtype=F32) / half)
    ang = positions.astype(F32)[:, None] * inv[None, :]
    cos = jnp.cos(ang)
    sin = jnp.sin(ang)
    return jnp.concatenate([cos, cos], axis=1), jnp.concatenate([-sin, sin], axis=1)


def _decay_tables(n_real):
    c = RET_CHUNK
    log_gamma = jnp.log(1.0 - 2.0 ** (-5.0 - jnp.arange(RET_HEADS, dtype=F32)))
    i = jnp.arange(c, dtype=F32)
    diff = i[:, None] - i[None, :]
    real = jnp.arange(c) < n_real
    dmat = jnp.where(diff >= 0, jnp.exp(log_gamma[:, None, None] * jnp.maximum(diff, 0.0)), 0.0)
    dmat = jnp.where((real[:, None] & real[None, :])[None], dmat, 0.0)
    q_decay = jnp.where(real[None, :], jnp.exp(log_gamma[:, None] * (i[None, :] + 1.0)), 0.0)
    k_decay = jnp.where(real[None, :], jnp.exp(log_gamma[:, None] * (n_real - 1.0 - i[None, :])), 0.0)
    c_decay = jnp.exp(log_gamma * n_real)
    rep = lambda t: jnp.broadcast_to(t[:, :, None], (RET_HEADS, c, LANES))
    cd = jnp.broadcast_to(c_decay[:, None, None], (RET_HEADS, SUBLANES, LANES))
    return dmat, rep(q_decay), rep(k_decay), cd


def _retention_body(q_ref, k_ref, v_ref, g_ref, cos_ref, sin_ref, dm_ref, qd_ref, kd_ref, cd_ref,
                    gg_ref, gb_ref, s0_ref, o_ref, s_ref, *, rows):
    @pl.when(pl.program_id(1) == 0)
    def _():
        s_ref[...] = s0_ref[...]

    cos = cos_ref[...]
    sin = sin_ref[...]

    def rot(x):
        return x * cos + pltpu.roll(x, RET_DIM // 2, 1) * sin

    def pad(x):
        if rows == RET_CHUNK:
            return x
        return jnp.concatenate([x, jnp.zeros((RET_CHUNK - rows, x.shape[1]), x.dtype)], axis=0)

    for h in range(RET_HEADS):
        hs = slice(h * RET_DIM, (h + 1) * RET_DIM)
        q = pad(rot(q_ref[:, hs]))
        k = pad(rot(k_ref[:, hs]) * (RET_DIM ** -0.5))
        v = pad(v_ref[:, hs])
        qb = q.astype(BF16)
        kb = k.astype(BF16)
        vb = v.astype(BF16)
        s = s_ref[0, h]
        att = lax.dot_general(qb, kb, NT_DIMS, preferred_element_type=F32) * dm_ref[h]
        o = jnp.dot(att.astype(BF16), vb, preferred_element_type=F32)
        o = o + jnp.dot(qb, s.astype(BF16), preferred_element_type=F32) * qd_ref[h]
        vdec = (v * kd_ref[h]).astype(BF16)
        s_ref[0, h] = s * cd_ref[h][0:1, :] + lax.dot_general(kb, vdec, TN_DIMS, preferred_element_type=F32)

        o = o[0:rows]
        mu = jnp.mean(o, axis=-1, keepdims=True)
        d = o - mu
        var = jnp.mean(d * d, axis=-1, keepdims=True)
        y = d * lax.rsqrt(var + EPS) * gg_ref[:, hs] + gb_ref[:, hs]
        g = g_ref[:, hs]
        o_ref[:, hs] = (g * jax.nn.sigmoid(g) * y).astype(BF16)


def _retention(z, s0, gn_gain, gn_bias, batch, t_pad, n_real, q_start):
    rows = min(t_pad, RET_CHUNK)
    nc = t_pad // rows
    assert n_real == t_pad or nc == 1, "padding rows are only supported in a single-chunk sequence"
    cos2, sin2 = _rotary_tables(q_start + jnp.arange(t_pad))
    dmat, qd, kd, cd = _decay_tables(min(n_real, rows))
    hb = RET_DIM
    rw = RET_WIDTH
    col = lambda base: (lambda b, c: (b * nc + c, base // rw))
    tab = lambda b, c: (0, 0, 0)
    o_r, s_new = pl.pallas_call(
        functools.partial(_retention_body, rows=rows),
        grid=(batch, nc),
        in_specs=[
            pl.BlockSpec((rows, rw), col(COL_RQ)),
            pl.BlockSpec((rows, rw), col(COL_RK)),
            pl.BlockSpec((rows, rw), col(COL_RV)),
            pl.BlockSpec((rows, rw), col(COL_RG)),
            pl.BlockSpec((rows, hb), lambda b, c: (c, 0)),
            pl.BlockSpec((rows, hb), lambda b, c: (c, 0)),
            pl.BlockSpec((RET_HEADS, RET_CHUNK, RET_CHUNK), tab),
            pl.BlockSpec((RET_HEADS, RET_CHUNK, LANES), tab),
            pl.BlockSpec((RET_HEADS, RET_CHUNK, LANES), tab),
            pl.BlockSpec((RET_HEADS, SUBLANES, LANES), tab),
            pl.BlockSpec((1, rw), lambda b, c: (0, 0)),
            pl.BlockSpec((1, rw), lambda b, c: (0, 0)),
            pl.BlockSpec((1, RET_HEADS, hb, hb), lambda b, c: (b, 0, 0, 0)),
        ],
        out_specs=[
            pl.BlockSpec((rows, rw), lambda b, c: (b * nc + c, 0)),
            pl.BlockSpec((1, RET_HEADS, hb, hb), lambda b, c: (b, 0, 0, 0)),
        ],
        out_shape=[
            jax.ShapeDtypeStruct((batch * t_pad, rw), BF16),
            jax.ShapeDtypeStruct((batch, RET_HEADS, hb, hb), F32),
        ],
        compiler_params=_params("arbitrary", "arbitrary"),
        name="retention",
    )(z, z, z, z, jnp.asarray(cos2), jnp.asarray(sin2), jnp.asarray(dmat), jnp.asarray(qd), jnp.asarray(kd),
      jnp.asarray(cd), gn_gain.reshape(1, rw), gn_bias.reshape(1, rw), s0)
    return o_r, s_new


def _conv_body(a_ref, g_ref, buf_ref, w_ref, b_ref, lg_ref, lb_ref, o_ref, nb_ref, ub_ref, uf_ref, *, tt, n_last):
    t = pl.program_id(1)

    def rounded(x):
        return x.astype(BF16).astype(F32)

    @pl.when(t == 0)
    def _():
        uf_ref[0:CONV_HALO, :] = buf_ref[0]
        ub_ref[0:CONV_HALO, :] = rounded(buf_ref[0])

    g = g_ref[...]
    u = a_ref[...] * jax.nn.sigmoid(g)
    uf_ref[CONV_HALO:CONV_HALO + tt, :] = u
    ub_ref[CONV_HALO:CONV_HALO + tt, :] = rounded(u)

    first = CONV_HALO - (CONV_K - 1)
    acc = jnp.broadcast_to(b_ref[...], (tt, CONV_WIDTH))
    for k in range(CONV_K):
        acc = acc + w_ref[k:k + 1, :] * ub_ref[first + k:first + k + tt, :]

    mu = jnp.mean(acc, axis=-1, keepdims=True)
    d = acc - mu
    var = jnp.mean(d * d, axis=-1, keepdims=True)
    y = d * lax.rsqrt(var + EPS) * lg_ref[...] + lb_ref[...]
    o_ref[...] = (y * jax.nn.sigmoid(y)).astype(BF16)

    @pl.when(t == pl.num_programs(1) - 1)
    def _():
        nb_ref[0] = uf_ref[n_last:n_last + CONV_HALO, :]

    ub_ref[0:CONV_HALO, :] = ub_ref[tt:tt + CONV_HALO, :]
    uf_ref[0:CONV_HALO, :] = uf_ref[tt:tt + CONV_HALO, :]


def _conv(z, buf, w, b, ln_g, ln_b, batch, t_pad, n_real):
    tt = min(t_pad, 512)
    nt = t_pad // tt
    n_last = n_real - (nt - 1) * tt
    first = CONV_HALO - (CONV_K - 1)
    buf_p = jnp.pad(buf, ((0, 0), (first, 0), (0, 0)))
    w_p = jnp.pad(w, ((0, CONV_HALO - CONV_K), (0, 0)))
    cw = CONV_WIDTH
    vec = lambda v: v.reshape(1, cw)
    o_c, nb = pl.pallas_call(
        functools.partial(_conv_body, tt=tt, n_last=n_last),
        grid=(batch, nt),
        in_specs=[
            pl.BlockSpec((tt, cw), lambda bb, t: (bb * nt + t, COL_CA // cw)),
            pl.BlockSpec((tt, cw), lambda bb, t: (bb * nt + t, COL_CG // cw)),
            pl.BlockSpec((1, CONV_HALO, cw), lambda bb, t: (bb, 0, 0)),
            pl.BlockSpec((CONV_HALO, cw), lambda bb, t: (0, 0)),
            pl.BlockSpec((1, cw), lambda bb, t: (0, 0)),
            pl.BlockSpec((1, cw), lambda bb, t: (0, 0)),
            pl.BlockSpec((1, cw), lambda bb, t: (0, 0)),
        ],
        out_specs=[
            pl.BlockSpec((tt, cw), lambda bb, t: (bb * nt + t, 0)),
            pl.BlockSpec((1, CONV_HALO, cw), lambda bb, t: (bb, 0, 0)),
        ],
        out_shape=[
            jax.ShapeDtypeStruct((batch * t_pad, cw), BF16),
            jax.ShapeDtypeStruct((batch, CONV_HALO, cw), F32),
        ],
        scratch_shapes=[pltpu.VMEM((CONV_HALO + tt, cw), F32), pltpu.VMEM((CONV_HALO + tt, cw), F32)],
        compiler_params=_params("arbitrary", "arbitrary"),
        name="conv",
    )(z, z, buf_p, w_p, vec(b), vec(ln_g), vec(ln_b))
    return o_c, nb[:, first:, :]


def _suffix_matrix(bk):
    j = np.arange(bk)[:, None]
    s = np.arange(bk)[None, :]
    m = np.concatenate([(j > s).astype(np.float32), np.ones((bk, LANES), np.float32)], axis=1)
    return jnp.asarray(m, dtype=BF16)


def _sb_tiles(qs, kblks, vblks, u, biases, carry, masked):
    n = len(qs)
    bq = qs[0].shape[0]
    bk = kblks[0].shape[0]
    zs, sps = [], []
    for j in range(n):
        s = lax.dot_general(qs[j], kblks[j], NT_DIMS, preferred_element_type=F32)
        z = s * (SB_DIM ** -0.5) + biases[j]
        zs.append(z)
        sps.append(jnp.maximum(z, 0.0) + jnp.log(1.0 + jnp.exp(-jnp.abs(z))))
    if masked:
        causal = lax.broadcasted_iota(I32, (bq, bk), 1) < lax.broadcasted_iota(I32, (bq, bk), 0)
    rts = []
    for j in range(n):
        spm = jnp.where(causal, sps[j], 0.0) if masked else sps[j]
        rts.append(jnp.dot(spm.astype(BF16), u, preferred_element_type=F32))
    out = []
    for j in range(n):
        c, acc = carry[2 * j], carry[2 * j + 1]
        c_all = c if bk == LANES else jnp.concatenate([c] * (bk // LANES), axis=1)
        a = jnp.exp((zs[j] - sps[j]) - (c_all + rts[j][:, :bk]))
        if masked:
            a = jnp.where(causal, a, 0.0)
        out += [c + rts[j][:, bk:], acc + jnp.dot(a.astype(BF16), vblks[j], preferred_element_type=F32)]
    return tuple(out)


def _sb_prompt_body(bias_ref, q_ref, k_ref, v_ref, u_ref, o_ref, kb_ref, vb_ref, *, bq, hp):
    hg = pl.program_id(1)
    i = pl.program_id(2)

    @pl.when(i == 0)
    def _():
        kb_ref[...] = k_ref[...].astype(BF16)
        vb_ref[...] = v_ref[...].astype(BF16)

    u = u_ref[...]
    cols = [slice(j * SB_DIM, (j + 1) * SB_DIM) for j in range(hp)]
    qs = [q_ref[:, cs].astype(BF16) for cs in cols]
    biases = [bias_ref[hg * hp + j] for j in range(hp)]

    def tiles(st, carry, masked):
        return _sb_tiles(qs, [kb_ref[pl.ds(st, bq), cs] for cs in cols], [vb_ref[pl.ds(st, bq), cs] for cs in cols],
                         u, biases, carry, masked)

    zero = jnp.zeros((bq, LANES), F32)
    carry = tiles(pl.multiple_of(i * bq, bq), (zero,) * (2 * hp), True)
    carry = lax.fori_loop(0, i, lambda n, cr: tiles(pl.multiple_of((i - 1 - n) * bq, bq), cr, False), carry)
    for j in range(hp):
        o_ref[:, cols[j]] = carry[2 * j + 1].astype(BF16)


def _sb_prompt(z, sb_bias, batch, t, bq=256, hp=4):
    nq = t // bq
    hw = hp * SB_DIM
    return pl.pallas_call(
        functools.partial(_sb_prompt_body, bq=bq, hp=hp),
        grid=(batch, SB_HEADS // hp, nq),
        in_specs=[
            pl.BlockSpec(memory_space=pltpu.SMEM),
            pl.BlockSpec((bq, hw), lambda b, h, i: (b * nq + i, COL_SQ // hw + h)),
            pl.BlockSpec((t, hw), lambda b, h, i: (b, COL_SK // hw + h)),
            pl.BlockSpec((t, hw), lambda b, h, i: (b, COL_SV // hw + h)),
            pl.BlockSpec((bq, bq + LANES), lambda b, h, i: (0, 0)),
        ],
        out_specs=pl.BlockSpec((bq, hw), lambda b, h, i: (b * nq + i, h)),
        out_shape=jax.ShapeDtypeStruct((batch * t, SB_WIDTH), BF16),
        scratch_shapes=[pltpu.VMEM((t, hw), BF16), pltpu.VMEM((t, hw), BF16)],
        compiler_params=_params("arbitrary", "arbitrary", "arbitrary"),
        name="sb_prompt",
    )(sb_bias, z, z, z, _suffix_matrix(bq))


SB_PAGES_PER_STEP = 8


def _sb_sample_body(pt_ref, bias_ref, q_ref, kn_ref, vn_ref, ck_hbm, cv_hbm, u_ref, o_ref,
                    kbuf, vbuf, sem, c_ref, acc_ref, *, layer, n_pages, pps):
    b = pl.program_id(0)
    s = pl.program_id(1)
    ns = pl.num_programs(1)
    step = b * ns + s
    slot = step % 2
    tp = SAMPLE_T_PAD

    def start_fetch(bb, ss, sl):
        for j in range(pps):
            page = pt_ref[bb, n_pages - 1 - (ss * pps + j)]
            for h in range(SB_HEADS):
                pltpu.make_async_copy(ck_hbm.at[layer, page, :, h, :], kbuf.at[sl, j, h], sem.at[0, sl]).start()
                pltpu.make_async_copy(cv_hbm.at[layer, page, :, h, :], vbuf.at[sl, j, h], sem.at[1, sl]).start()

    def wait_fetch(sl):
        pltpu.make_async_copy(kbuf.at[sl], kbuf.at[sl], sem.at[0, sl]).wait()
        pltpu.make_async_copy(vbuf.at[sl], vbuf.at[sl], sem.at[1, sl]).wait()

    @pl.when(step == 0)
    def _():
        start_fetch(0, 0, 0)

    @pl.when(step + 1 < pl.num_programs(0) * ns)
    def _():
        last = s == ns - 1
        start_fetch(jnp.where(last, b + 1, b), jnp.where(last, 0, s + 1), 1 - slot)

    u = u_ref[...]
    qs = [q_ref[:, h * SB_DIM:(h + 1) * SB_DIM].astype(BF16) for h in range(SB_HEADS)]
    bias = jnp.concatenate([jnp.full((tp, LANES), bias_ref[h], F32) for h in range(SB_HEADS)], axis=0)

    def attend(n_blocks, head_k, head_v, c, acc, masked):
        zs, sps, rts = [], [], []
        for j in range(n_blocks):
            sc = jnp.concatenate([lax.dot_general(qs[h], head_k(j, h).astype(BF16), NT_DIMS, preferred_element_type=F32)
                                  for h in range(SB_HEADS)], axis=0)
            z = sc * (SB_DIM ** -0.5) + bias
            zs.append(z)
            sps.append(jnp.maximum(z, 0.0) + jnp.log(1.0 + jnp.exp(-jnp.abs(z))))
        if masked:
            t = lax.broadcasted_iota(I32, zs[0].shape, 0) & (tp - 1)
            causal = lax.broadcasted_iota(I32, zs[0].shape, 1) < t
        for j in range(n_blocks):
            spm = jnp.where(causal, sps[j], 0.0) if masked else sps[j]
            rts.append(jnp.dot(spm.astype(BF16), u, preferred_element_type=F32))
        pvs = []
        for j in range(n_blocks):
            a = jnp.exp((zs[j] - sps[j]) - (c + rts[j][:, :PAGE_SIZE]))
            if masked:
                a = jnp.where(causal, a, 0.0)
            ab = a.astype(BF16)
            pvs.append(jnp.concatenate([jnp.dot(ab[h * tp:(h + 1) * tp], head_v(j, h).astype(BF16),
                                                preferred_element_type=F32) for h in range(SB_HEADS)], axis=0))
            c = c + rts[j][:, PAGE_SIZE:]
        return c, acc + functools.reduce(jnp.add, pvs)

    @pl.when(s == 0)
    def _():
        zero = jnp.zeros((SB_HEADS * tp, LANES), F32)
        c, acc = attend(1, lambda j, h: kn_ref[0, :, h * SB_DIM:(h + 1) * SB_DIM],
                        lambda j, h: vn_ref[0, :, h * SB_DIM:(h + 1) * SB_DIM], zero, zero, True)
        c_ref[...] = c
        acc_ref[...] = acc

    wait_fetch(slot)
    c, acc = attend(pps, lambda j, h: kbuf[slot, j, h], lambda j, h: vbuf[slot, j, h], c_ref[...], acc_ref[...], False)
    c_ref[...] = c
    acc_ref[...] = acc

    @pl.when(s == ns - 1)
    def _():
        for h in range(SB_HEADS):
            o_ref[:, h * SB_DIM:(h + 1) * SB_DIM] = acc[h * tp:(h + 1) * tp].astype(BF16)


def _sb_sample(z, cache_k, cache_v, layer, page_table, sb_bias, batch):
    n_pages = page_table.shape[1]
    pps = SB_PAGES_PER_STEP
    assert n_pages % pps == 0
    tp = SAMPLE_T_PAD
    w = SB_WIDTH

    def new_page(col):
        rows = z[:, col:col + w].reshape(batch, tp, w)
        return jnp.pad(rows, ((0, 0), (0, PAGE_SIZE - tp), (0, 0)))

    page_buf = pltpu.VMEM((2, pps, SB_HEADS, PAGE_SIZE, SB_DIM), F32)
    grid_spec = pltpu.PrefetchScalarGridSpec(
        num_scalar_prefetch=1,
        grid=(batch, n_pages // pps),
        in_specs=[
            pl.BlockSpec(memory_space=pltpu.SMEM),
            pl.BlockSpec((tp, w), lambda b, s, pt: (b, COL_SQ // w)),
            pl.BlockSpec((1, PAGE_SIZE, w), lambda b, s, pt: (b, 0, 0)),
            pl.BlockSpec((1, PAGE_SIZE, w), lambda b, s, pt: (b, 0, 0)),
            pl.BlockSpec(memory_space=pl.ANY),
            pl.BlockSpec(memory_space=pl.ANY),
            pl.BlockSpec((PAGE_SIZE, PAGE_SIZE + LANES), lambda b, s, pt: (0, 0)),
        ],
        out_specs=pl.BlockSpec((tp, w), lambda b, s, pt: (b, 0)),
        scratch_shapes=[page_buf, page_buf, pltpu.SemaphoreType.DMA((2, 2)),
                        pltpu.VMEM((SB_HEADS * tp, LANES), F32), pltpu.VMEM((SB_HEADS * tp, LANES), F32)],
    )
    return pl.pallas_call(
        functools.partial(_sb_sample_body, layer=layer, n_pages=n_pages, pps=pps),
        grid_spec=grid_spec,
        out_shape=jax.ShapeDtypeStruct((batch * tp, w), BF16),
        compiler_params=_params("arbitrary", "arbitrary"),
        name="sb_sample",
    )(page_table, sb_bias, z, new_page(COL_SK), new_page(COL_SV), cache_k, cache_v, _suffix_matrix(PAGE_SIZE))


ROUTE_ROWS = SUBLANES
ROUTER_PAD = LANES


def _route_rows(lt):
    lg = [lt[g:g + 1, :] for g in range(N_GROUPS)]
    m = functools.reduce(jnp.maximum, lg)
    gi = jnp.full(m.shape, N_GROUPS - 1, I32)
    for g in range(N_GROUPS - 2, -1, -1):
        gi = jnp.where(lg[g] == m, g, gi)
    den = functools.reduce(jnp.add, [jnp.exp(x - m) for x in lg])
    p_group = 1.0 / den

    def expert_logit(k):
        out = lt[N_GROUPS + (N_GROUPS - 1) * EXPERTS_PER_GROUP + k:N_GROUPS + (N_GROUPS - 1) * EXPERTS_PER_GROUP + k + 1, :]
        for g in range(N_GROUPS - 2, -1, -1):
            r = N_GROUPS + g * EXPERTS_PER_GROUP + k
            out = jnp.where(gi == g, lt[r:r + 1, :], out)
        return out

    le = [expert_logit(k) for k in range(EXPERTS_PER_GROUP)]
    v1 = functools.reduce(jnp.maximum, le)
    i1 = jnp.full(m.shape, EXPERTS_PER_GROUP - 1, I32)
    for k in range(EXPERTS_PER_GROUP - 2, -1, -1):
        i1 = jnp.where(le[k] == v1, k, i1)
    rest = [jnp.where(i1 == k, -jnp.inf, le[k]) for k in range(EXPERTS_PER_GROUP)]
    v2 = functools.reduce(jnp.maximum, rest)
    i2 = jnp.full(m.shape, EXPERTS_PER_GROUP - 1, I32)
    for k in range(EXPERTS_PER_GROUP - 2, -1, -1):
        i2 = jnp.where(rest[k] == v2, k, i2)
    e2 = jnp.exp(v2 - v1)
    inv = 1.0 / (1.0 + e2)
    rows = [p_group * inv, p_group * (e2 * inv),
            (gi * EXPERTS_PER_GROUP + i1).astype(F32), (gi * EXPERTS_PER_GROUP + i2).astype(F32)]
    rows += [jnp.zeros_like(m)] * (ROUTE_ROWS - len(rows))
    return jnp.concatenate(rows, axis=0)


def _out_proj_body(x_ref, or_ref, oc_ref, os_ref, w_ref, g_ref, wr_ref, br_ref, x2_ref, h2_ref, rt_ref):
    y = jnp.dot(or_ref[...], w_ref[0:RET_WIDTH, :], preferred_element_type=F32)
    y = y + jnp.dot(oc_ref[...], w_ref[RET_WIDTH:RET_WIDTH + CONV_WIDTH, :], preferred_element_type=F32)
    y = y + jnp.dot(os_ref[...], w_ref[RET_WIDTH + CONV_WIDTH:, :], preferred_element_type=F32)
    x2 = x_ref[...] + y
    x2_ref[...] = x2
    ms = jnp.mean(x2 * x2, axis=-1, keepdims=True)
    h = x2 * lax.rsqrt(ms + EPS) * g_ref[...]
    h2_ref[...] = h
    lt = lax.dot_general(wr_ref[...], h.astype(BF16), NT_DIMS, preferred_element_type=F32)
    rt_ref[...] = _route_rows(lt + br_ref[...])


def _out_proj(x, o_r, o_c, o_s, w_out_bf16, norm_ffn, rg_w, rg_b, re_w, re_b, tm):
    n, d = x.shape
    wr = jnp.concatenate([rg_w, re_w], axis=1).T
    wr = jnp.pad(wr, ((0, ROUTER_PAD - wr.shape[0]), (0, 0))).astype(BF16)
    br = jnp.pad(jnp.concatenate([rg_b, re_b]), (0, ROUTER_PAD - N_GROUPS - N_EXPERTS)).reshape(ROUTER_PAD, 1)
    row = lambda width: pl.BlockSpec((tm, width), lambda i: (i, 0))
    full = lambda a, b: pl.BlockSpec((a, b), lambda i: (0, 0))
    return pl.pallas_call(
        _out_proj_body,
        grid=(n // tm,),
        in_specs=[row(d), row(RET_WIDTH), row(CONV_WIDTH), row(SB_WIDTH), full(d, d), full(1, d),
                  full(ROUTER_PAD, d), full(ROUTER_PAD, 1)],
        out_specs=[row(d), row(d), pl.BlockSpec((ROUTE_ROWS, tm), lambda i: (0, i))],
        out_shape=[jax.ShapeDtypeStruct((n, d), F32), jax.ShapeDtypeStruct((n, d), F32),
                   jax.ShapeDtypeStruct((ROUTE_ROWS, n), F32)],
        compiler_params=_params("arbitrary"),
        name="out_proj",
    )(x, o_r, o_c, o_s, w_out_bf16, norm_ffn.reshape(1, d), wr, br)


def _moe_plan(route, n_tok, bm):
    gates = route[0:TOP_K, :]
    experts = route[TOP_K:2 * TOP_K, :].astype(I32)
    a = TOP_K * n_tok
    n_blocks = (a + N_EXPERTS * (bm - 1) + bm - 1) // bm
    r = n_blocks * bm
    flat_e = experts.reshape(-1)
    onehot = (flat_e[:, None] == jnp.arange(N_EXPERTS, dtype=I32)[None, :]).astype(I32)
    rank = jnp.sum((jnp.cumsum(onehot, axis=0) - onehot) * onehot, axis=1)
    counts = jnp.sum(onehot, axis=0)
    padded = (counts + bm - 1) // bm * bm
    pad_ends = jnp.cumsum(padded)
    pad_starts = pad_ends - padded
    dest = pad_starts[flat_e] + rank
    slot_of_row = jnp.full((r,), -1, I32).at[dest].set(jnp.arange(a, dtype=I32))
    is_pad = slot_of_row < 0
    row_dst = jnp.maximum(slot_of_row, 0)
    row_tok = jnp.where(row_dst >= n_tok, row_dst - n_tok, row_dst)
    row_gate = jnp.where(is_pad, 0.0, gates.reshape(-1)[row_dst])
    n_used = (pad_ends[-1] // bm).astype(I32)
    blk = jnp.minimum(jnp.arange(n_blocks, dtype=I32), n_used - 1) * bm
    blk_expert = jnp.minimum(jnp.searchsorted(pad_ends, blk, side='right'), N_EXPERTS - 1).astype(I32)
    blk_valid = jnp.clip(pad_starts[blk_expert] + counts[blk_expert] - blk, 0, bm).astype(I32)
    gate_rows = jnp.broadcast_to(row_gate[:, None], (r, LANES))
    return row_tok, row_dst, blk_expert, blk_valid, n_used.reshape(1), gate_rows, n_blocks


def _moe_body(tok_ref, dst_ref, be_ref, bv_ref, nu_ref, h_hbm, gate_ref, wg_ref, wu_ref, wd_ref, y_hbm,
              xbuf, ybuf, wgb, wub, wdb, gsem, ssem, *, bm):
    i = pl.program_id(0)
    n_used = nu_ref[0]
    slot = i % 2

    def start_gather(blk, s):
        def group(g, carry):
            for r in range(SUBLANES):
                tok = tok_ref[blk * bm + g * SUBLANES + r]
                pltpu.make_async_copy(h_hbm.at[pl.ds(tok, 1), :], xbuf.at[s, g, pl.ds(r, 1), :], gsem.at[s]).start()
            return carry
        lax.fori_loop(0, bm // SUBLANES, group, 0)

    def wait_gather(s):
        pltpu.make_async_copy(xbuf.at[s], xbuf.at[s], gsem.at[s]).wait()

    def start_scatter(blk, s):
        def row(g, r):
            dst = dst_ref[blk * bm + g * SUBLANES + r]
            pltpu.make_async_copy(ybuf.at[s, g, pl.ds(r, 1), :], y_hbm.at[pl.ds(dst, 1), :], ssem.at[s]).start()

        nv = bv_ref[blk]

        def group(g, carry):
            for r in range(SUBLANES):
                row(g, r)
            return carry

        def single(j, carry):
            row(j // SUBLANES, j % SUBLANES)
            return carry

        full = nv // SUBLANES
        lax.fori_loop(0, full, group, 0)
        lax.fori_loop(full * SUBLANES, nv, single, 0)

    def wait_scatter(blk, s):
        nv = bv_ref[blk]
        p = 1
        while p <= bm:
            chunk = ybuf.at[s, 0, pl.ds(0, p), :] if p < SUBLANES else ybuf.at[s, pl.ds(0, p // SUBLANES)]

            @pl.when((nv & p) != 0)
            def _(chunk=chunk):
                pltpu.make_async_copy(chunk, chunk, ssem.at[s]).wait()
            p *= 2

    @pl.when(i == 0)
    def _():
        start_gather(0, 0)

    @pl.when(i + 1 < n_used)
    def _():
        start_gather(i + 1, 1 - slot)

    @pl.when(i < n_used)
    def _():
        wait_gather(slot)

        @pl.when(i >= 2)
        def _():
            wait_scatter(i - 2, slot)

        @pl.when((i == 0) | (be_ref[i] != be_ref[jnp.maximum(i - 1, 0)]))
        def _():
            wgb[...] = wg_ref[0, 0].astype(BF16)
            wub[...] = wu_ref[0, 0].astype(BF16)
            wdb[...] = wd_ref[0, 0].astype(BF16)

        x = xbuf[slot].reshape(bm, D_MODEL).astype(BF16)
        g = jnp.dot(x, wgb[...], preferred_element_type=F32)
        u = jnp.dot(x, wub[...], preferred_element_type=F32)
        hm = (g * jax.nn.sigmoid(g) * u).astype(BF16)
        y = jnp.dot(hm, wdb[...], preferred_element_type=F32)
        gate = gate_ref[...].astype(BF16).astype(F32)
        y = y.astype(BF16).astype(F32) * jnp.concatenate([gate] * (D_MODEL // LANES), axis=1)
        ybuf[slot] = y.reshape(bm // SUBLANES, SUBLANES, D_MODEL)
        start_scatter(i, slot)

    @pl.when(i == pl.num_programs(0) - 1)
    def _():
        wait_scatter(n_used - 1, (n_used - 1) % 2)

        @pl.when(n_used >= 2)
        def _():
            wait_scatter(n_used - 2, n_used % 2)


def _moe(h2, route, w_gate, w_up, w_down, layer, bm):
    n_tok, d = h2.shape
    row_tok, row_dst, blk_expert, blk_valid, n_used, gate_rows, n_blocks = _moe_plan(route, n_tok, bm)
    wspec = lambda a, b: pl.BlockSpec((1, 1, a, b), lambda i, tok, dst, be, bv, nu: (layer, be[i], 0, 0))
    grid_spec = pltpu.PrefetchScalarGridSpec(
        num_scalar_prefetch=5,
        grid=(n_blocks,),
        in_specs=[
            pl.BlockSpec(memory_space=pl.ANY),
            pl.BlockSpec((bm, LANES), lambda i, tok, dst, be, bv, nu: (i, 0)),
            wspec(d, D_EXPERT), wspec(d, D_EXPERT), wspec(D_EXPERT, d),
        ],
        out_specs=pl.BlockSpec(memory_space=pl.ANY),
        scratch_shapes=[pltpu.VMEM((2, bm // SUBLANES, SUBLANES, d), F32), pltpu.VMEM((2, bm // SUBLANES, SUBLANES, d), F32),
                        pltpu.VMEM((d, D_EXPERT), BF16), pltpu.VMEM((d, D_EXPERT), BF16), pltpu.VMEM((D_EXPERT, d), BF16),
                        pltpu.SemaphoreType.DMA((2,)), pltpu.SemaphoreType.DMA((2,))],
    )
    return pl.pallas_call(
        functools.partial(_moe_body, bm=bm),
        grid_spec=grid_spec,
        out_shape=jax.ShapeDtypeStruct((TOP_K * n_tok, d), F32),
        compiler_params=_params("arbitrary"),
        name="moe",
    )(row_tok, row_dst, blk_expert, blk_valid, n_used, h2, gate_rows, w_gate, w_up, w_down)


def _combine_body(x_ref, y0_ref, y1_ref, g_ref, o_ref, *, final_norm):
    x = x_ref[...] + y0_ref[...] + y1_ref[...]
    if final_norm:
        ms = jnp.mean(x * x, axis=-1, keepdims=True)
        x = x * lax.rsqrt(ms + EPS) * g_ref[...]
    o_ref[...] = x


def _combine(x2, y, norm_gain, final_norm, tm):
    n, d = x2.shape
    nb = n // tm
    return pl.pallas_call(
        functools.partial(_combine_body, final_norm=final_norm),
        grid=(nb,),
        in_specs=[
            pl.BlockSpec((tm, d), lambda i: (i, 0)),
            pl.BlockSpec((tm, d), lambda i: (i, 0)),
            pl.BlockSpec((tm, d), lambda i: (i + nb, 0)),
            pl.BlockSpec((1, d), lambda i: (0, 0)),
        ],
        out_specs=pl.BlockSpec((tm, d), lambda i: (i, 0)),
        out_shape=jax.ShapeDtypeStruct((n, d), F32),
        compiler_params=_params("arbitrary"),
        name="combine",
    )(x2, y, y, norm_gain.reshape(1, d))


def _layer(x, l, last, group, w, state):
    batch, t_pad, n_real = group['batch'], group['t_pad'], group['n_real']
    if group['paged']:
        z = _norm_proj(x, w['norm_mix'][l], w['w_in'], l, group['tm'])
        k_new = z[:, COL_SK:COL_SK + SB_WIDTH].reshape(batch, t_pad, SB_HEADS, SB_DIM)[:, :n_real]
        v_new = z[:, COL_SV:COL_SV + SB_WIDTH].reshape(batch, t_pad, SB_HEADS, SB_DIM)[:, :n_real]
    else:
        z, k_new, v_new = _norm_proj(x, w['norm_mix'][l], w['w_in'], l, group['tm'], seq_len=t_pad)
    o_r, s_ret = _retention(z, state['ret'], w['ret_gn_gain'][l], w['ret_gn_bias'][l], batch, t_pad, n_real, group['q_start'])
    o_c, new_buf = _conv(z, state['conv'], w['conv_w'][l], w['conv_b'][l], w['conv_ln_gain'][l], w['conv_ln_bias'][l],
                         batch, t_pad, n_real)
    if group['paged']:
        o_s = _sb_sample(z, state['cache_k'], state['cache_v'], l, state['page_table'], w['sb_bias'][l], batch)
    else:
        o_s = _sb_prompt(z, w['sb_bias'][l], batch, t_pad)
    x2, h2, route = _out_proj(x, o_r, o_c, o_s, w['w_out'][l], w['norm_ffn'][l], w['router_group_w'][l],
                              w['router_group_b'][l], w['router_expert_w'][l], w['router_expert_b'][l], group['tm_out'])
    y = _moe(h2, route, w['w_gate'], w['w_up'], w['w_down'], l, group['bm'])
    x3 = _combine(x2, y, w['norm_final'], last, group['tm_out'])
    return x3, s_ret, new_buf, k_new, v_new


def kernel(x_prompt, x_sample, state_ret, state_conv, cache_k, cache_v, page_table, norm_mix, w_in, ret_gn_gain, ret_gn_bias, conv_w, conv_b, conv_ln_gain, conv_ln_bias, sb_bias, w_out, norm_ffn, router_group_w, router_group_b, router_expert_w, router_expert_b, w_gate, w_up, w_down, norm_final):
    b, t, d = x_prompt.shape
    db, dt, _ = x_sample.shape
    depth = w_in.shape[0]
    n_past = page_table.shape[1] * PAGE_SIZE
    w = dict(norm_mix=norm_mix, w_in=w_in.astype(BF16), ret_gn_gain=ret_gn_gain, ret_gn_bias=ret_gn_bias,
             conv_w=conv_w, conv_b=conv_b, conv_ln_gain=conv_ln_gain, conv_ln_bias=conv_ln_bias, sb_bias=sb_bias,
             w_out=w_out.astype(BF16), norm_ffn=norm_ffn, router_group_w=router_group_w, router_group_b=router_group_b,
             router_expert_w=router_expert_w, router_expert_b=router_expert_b,
             w_gate=w_gate, w_up=w_up, w_down=w_down, norm_final=norm_final)
    prompt = dict(batch=b, t_pad=t, n_real=t, q_start=0, tm=1024, tm_out=256, bm=256, paged=False)
    sample = dict(batch=db, t_pad=SAMPLE_T_PAD, n_real=dt, q_start=n_past, tm=db * SAMPLE_T_PAD,
                  tm_out=db * SAMPLE_T_PAD, bm=SUBLANES, paged=True)

    xp = x_prompt.reshape(b * t, d)
    xs = jnp.pad(x_sample, ((0, 0), (0, SAMPLE_T_PAD - dt), (0, 0))).reshape(db * SAMPLE_T_PAD, d)
    outs_p, outs_s = [], []
    for l in range(depth):
        last = l == depth - 1
        st_p = dict(ret=jnp.zeros((b, RET_HEADS, RET_DIM, RET_DIM), F32), conv=jnp.zeros((b, CONV_K - 1, CONV_WIDTH), F32))
        xp, *lp = _layer(xp, l, last, prompt, w, st_p)
        outs_p.append(lp)
        st_s = dict(ret=state_ret[l], conv=state_conv[l], cache_k=cache_k, cache_v=cache_v, page_table=page_table)
        xs, *ls = _layer(xs, l, last, sample, w, st_s)
        outs_s.append(ls)
    y_prompt = xp.reshape(b, t, d)
    y_sample = xs.reshape(db, SAMPLE_T_PAD, d)[:, :dt]
    stack = lambda outs, j: jnp.stack([o[j] for o in outs])
    return (y_prompt, y_sample,
            stack(outs_p, 0), stack(outs_p, 1), stack(outs_p, 2), stack(outs_p, 3),
            stack(outs_s, 0), stack(outs_s, 1), stack(outs_s, 2), stack(outs_s, 3))
```

```python
import functools

import numpy as np
import jax
import jax.numpy as jnp
from jax import lax
from jax.experimental import pallas as pl
from jax.experimental.pallas import tpu as pltpu

F32 = jnp.float32
BF16 = jnp.bfloat16
I32 = jnp.int32

D_MODEL = 2048
RET_HEADS = 4
RET_DIM = 128
RET_WIDTH = RET_HEADS * RET_DIM
RET_CHUNK = 128
ROPE_BASE = 10000.0
CONV_WIDTH = 512
CONV_K = 31
SB_HEADS = 8
SB_DIM = 128
SB_WIDTH = SB_HEADS * SB_DIM
IN_WIDTH = 4 * RET_WIDTH + 2 * CONV_WIDTH + 3 * SB_WIDTH
N_GROUPS = 4
EXPERTS_PER_GROUP = 8
N_EXPERTS = N_GROUPS * EXPERTS_PER_GROUP
TOP_K = 2
D_EXPERT = 512
PAGE_SIZE = 128
EPS = 1e-6

COL_RQ, COL_RK, COL_RV, COL_RG = 0, 512, 1024, 1536
COL_CA, COL_CG = 2048, 2560
COL_SQ, COL_SK, COL_SV = 3072, 4096, 5120

LANES = 128
SUBLANES = 8
SAMPLE_T_PAD = SUBLANES
CONV_HALO = 32
VMEM_LIMIT = 56 * 1024 * 1024

NT_DIMS = (((1,), (1,)), ((), ()))
TN_DIMS = (((0,), (0,)), ((), ()))


def _params(*sem):
    return pltpu.CompilerParams(dimension_semantics=sem, vmem_limit_bytes=VMEM_LIMIT)


def _norm_proj_body(x_ref, g_ref, w_ref, z_ref, *rest, tm, tn, seq_len):
    i = pl.program_id(0)
    j = pl.program_id(1)
    if seq_len is None:
        (h_ref,) = rest
    else:
        k_hbm, v_hbm, h_ref, stage, sem = rest

    @pl.when(j == 0)
    def _():
        x = x_ref[...]
        ms = jnp.mean(x * x, axis=-1, keepdims=True)
        h_ref[...] = (x * lax.rsqrt(ms + EPS) * g_ref[...]).astype(BF16)

    z = jnp.dot(h_ref[...], w_ref[0], preferred_element_type=F32)
    z_ref[...] = z
    if seq_len is None:
        return

    heads_per_tile = tn // SB_DIM
    n_col_tiles = IN_WIDTH // tn
    first_kv = COL_SK // tn
    kv_tiles = [(first_kv + t, k_hbm, t * heads_per_tile) for t in range(SB_WIDTH // tn)]
    kv_tiles += [(COL_SV // tn + t, v_hbm, t * heads_per_tile) for t in range(SB_WIDTH // tn)]
    tiles_per_seq = seq_len // tm
    b = i // tiles_per_seq
    t0 = pl.multiple_of((i % tiles_per_seq) * tm, tm)

    def wait_tile():
        pltpu.make_async_copy(stage, stage, sem).wait()

    @pl.when((j > first_kv) | ((j == 0) & (i > 0)))
    def _():
        wait_tile()

    for jt, dst, head0 in kv_tiles:
        @pl.when(j == jt)
        def _(dst=dst, head0=head0):
            stage[...] = z
            for hh in range(heads_per_tile):
                pltpu.make_async_copy(stage.at[:, pl.ds(hh * SB_DIM, SB_DIM)],
                                      dst.at[b, pl.ds(t0, tm), head0 + hh, :], sem).start()

    @pl.when((i == pl.num_programs(0) - 1) & (j == n_col_tiles - 1))
    def _():
        wait_tile()


def _norm_proj(x, gain, w_bf16, layer, tm, tn=512, seq_len=None):
    n, d = x.shape
    dout = w_bf16.shape[2]
    kv = seq_len is not None
    if kv:
        assert COL_SV + SB_WIDTH == dout and COL_SV == COL_SK + SB_WIDTH and SB_WIDTH % tn == 0 and seq_len % tm == 0
    kv_shape = jax.ShapeDtypeStruct((n // seq_len, seq_len, SB_HEADS, SB_DIM), F32) if kv else None
    out = pl.pallas_call(
        functools.partial(_norm_proj_body, tm=tm, tn=tn, seq_len=seq_len),
        grid=(n // tm, dout // tn),
        in_specs=[
            pl.BlockSpec((tm, d), lambda i, j: (i, 0)),
            pl.BlockSpec((1, d), lambda i, j: (0, 0)),
            pl.BlockSpec((1, d, tn), lambda i, j: (layer, 0, j)),
        ],
        out_specs=[pl.BlockSpec((tm, tn), lambda i, j: (i, j))] + ([pl.BlockSpec(memory_space=pl.ANY)] * 2 if kv else []),
        out_shape=[jax.ShapeDtypeStruct((n, dout), F32)] + ([kv_shape, kv_shape] if kv else []),
        scratch_shapes=[pltpu.VMEM((tm, d), BF16)] + ([pltpu.VMEM((tm, tn), F32), pltpu.SemaphoreType.DMA(())] if kv else []),
        compiler_params=_params("arbitrary", "arbitrary"),
        name="norm_proj",
    )(x, gain.reshape(1, d), w_bf16)
    return out if kv else out[0]


def _rotary_tables(positions):
    half = RET_DIM // 2
    inv = ROPE_BASE ** (-jnp.arange(half, dtype=F32) / half)
    ang = positions.astype(F32)[:, None] * inv[None, :]
    cos = jnp.cos(ang)
    sin = jnp.sin(ang)
    return jnp.concatenate([cos, cos], axis=1), jnp.concatenate([-sin, sin], axis=1)


def _decay_tables(n_real):
    c = RET_CHUNK
    log_gamma = jnp.log(1.0 - 2.0 ** (-5.0 - jnp.arange(RET_HEADS, dtype=F32)))
    i = jnp.arange(c, dtype=F32)
    diff = i[:, None] - i[None, :]
    real = jnp.arange(c) < n_real
    dmat = jnp.where(diff >= 0, jnp.exp(log_gamma[:, None, None] * jnp.maximum(diff, 0.0)), 0.0)
    dmat = jnp.where((real[:, None] & real[None, :])[None], dmat, 0.0)
    q_decay = jnp.where(real[None, :], jnp.exp(log_gamma[:, None] * (i[None, :] + 1.0)), 0.0)
    k_decay = jnp.where(real[None, :], jnp.exp(log_gamma[:, None] * (n_real - 1.0 - i[None, :])), 0.0)
    c_decay = jnp.exp(log_gamma * n_real)
    rep = lambda t: jnp.broadcast_to(t[:, :, None], (RET_HEADS, c, LANES))
    cd = jnp.broadcast_to(c_decay[:, None, None], (RET_HEADS, SUBLANES, LANES))
    return dmat, rep(q_decay), rep(k_decay), cd


def _retention_body(q_ref, k_ref, v_ref, g_ref, cos_ref, sin_ref, dm_ref, qd_ref, kd_ref, cd_ref,
                    gg_ref, gb_ref, s0_ref, o_ref, s_ref, *, rows):
    @pl.when(pl.program_id(1) == 0)
    def _():
        s_ref[...] = s0_ref[...]

    cos = cos_ref[...]
    sin = sin_ref[...]

    def rot(x):
        return x * cos + pltpu.roll(x, RET_DIM // 2, 1) * sin

    def pad(x):
        if rows == RET_CHUNK:
            return x
        return jnp.concatenate([x, jnp.zeros((RET_CHUNK - rows, x.shape[1]), x.dtype)], axis=0)

    for h in range(RET_HEADS):
        hs = slice(h * RET_DIM, (h + 1) * RET_DIM)
        q = pad(rot(q_ref[:, hs]))
        k = pad(rot(k_ref[:, hs]) * (RET_DIM ** -0.5))
        v = pad(v_ref[:, hs])
        qb = q.astype(BF16)
        kb = k.astype(BF16)
        vb = v.astype(BF16)
        s = s_ref[0, h]
        att = lax.dot_general(qb, kb, NT_DIMS, preferred_element_type=F32) * dm_ref[h]
        o = jnp.dot(att.astype(BF16), vb, preferred_element_type=F32)
        o = o + jnp.dot(qb, s.astype(BF16), preferred_element_type=F32) * qd_ref[h]
        vdec = (v * kd_ref[h]).astype(BF16)
        s_ref[0, h] = s * cd_ref[h][0:1, :] + lax.dot_general(kb, vdec, TN_DIMS, preferred_element_type=F32)

        o = o[0:rows]
        mu = jnp.mean(o, axis=-1, keepdims=True)
        d = o - mu
        var = jnp.mean(d * d, axis=-1, keepdims=True)
        y = d * lax.rsqrt(var + EPS) * gg_ref[:, hs] + gb_ref[:, hs]
        g = g_ref[:, hs]
        o_ref[:, hs] = (g * jax.nn.sigmoid(g) * y).astype(BF16)


def _retention(z, s0, gn_gain, gn_bias, batch, t_pad, n_real, q_start):
    rows = min(t_pad, RET_CHUNK)
    nc = t_pad // rows
    assert n_real == t_pad or nc == 1, "padding rows are only supported in a single-chunk sequence"
    cos2, sin2 = _rotary_tables(q_start + jnp.arange(t_pad))
    dmat, qd, kd, cd = _decay_tables(min(n_real, rows))
    hb = RET_DIM
    rw = RET_WIDTH
    col = lambda base: (lambda b, c: (b * nc + c, base // rw))
    tab = lambda b, c: (0, 0, 0)
    o_r, s_new = pl.pallas_call(
        functools.partial(_retention_body, rows=rows),
        grid=(batch, nc),
        in_specs=[
            pl.BlockSpec((rows, rw), col(COL_RQ)),
            pl.BlockSpec((rows, rw), col(COL_RK)),
            pl.BlockSpec((rows, rw), col(COL_RV)),
            pl.BlockSpec((rows, rw), col(COL_RG)),
            pl.BlockSpec((rows, hb), lambda b, c: (c, 0)),
            pl.BlockSpec((rows, hb), lambda b, c: (c, 0)),
            pl.BlockSpec((RET_HEADS, RET_CHUNK, RET_CHUNK), tab),
            pl.BlockSpec((RET_HEADS, RET_CHUNK, LANES), tab),
            pl.BlockSpec((RET_HEADS, RET_CHUNK, LANES), tab),
            pl.BlockSpec((RET_HEADS, SUBLANES, LANES), tab),
            pl.BlockSpec((1, rw), lambda b, c: (0, 0)),
            pl.BlockSpec((1, rw), lambda b, c: (0, 0)),
            pl.BlockSpec((1, RET_HEADS, hb, hb), lambda b, c: (b, 0, 0, 0)),
        ],
        out_specs=[
            pl.BlockSpec((rows, rw), lambda b, c: (b * nc + c, 0)),
            pl.BlockSpec((1, RET_HEADS, hb, hb), lambda b, c: (b, 0, 0, 0)),
        ],
        out_shape=[
            jax.ShapeDtypeStruct((batch * t_pad, rw), BF16),
            jax.ShapeDtypeStruct((batch, RET_HEADS, hb, hb), F32),
        ],
        compiler_params=_params("arbitrary", "arbitrary"),
        name="retention",
    )(z, z, z, z, jnp.asarray(cos2), jnp.asarray(sin2), jnp.asarray(dmat), jnp.asarray(qd), jnp.asarray(kd),
      jnp.asarray(cd), gn_gain.reshape(1, rw), gn_bias.reshape(1, rw), s0)
    return o_r, s_new


def _conv_body(a_ref, g_ref, buf_ref, w_ref, b_ref, lg_ref, lb_ref, o_ref, nb_ref, ub_ref, uf_ref, *, tt, n_last):
    t = pl.program_id(1)

    def rounded(x):
        return x.astype(BF16).astype(F32)

    @pl.when(t == 0)
    def _():
        uf_ref[0:CONV_HALO, :] = buf_ref[0]
        ub_ref[0:CONV_HALO, :] = rounded(buf_ref[0])

    g = g_ref[...]
    u = a_ref[...] * jax.nn.sigmoid(g)
    uf_ref[CONV_HALO:CONV_HALO + tt, :] = u
    ub_ref[CONV_HALO:CONV_HALO + tt, :] = rounded(u)

    first = CONV_HALO - (CONV_K - 1)
    acc = jnp.broadcast_to(b_ref[...], (tt, CONV_WIDTH))
    for k in range(CONV_K):
        acc = acc + w_ref[k:k + 1, :] * ub_ref[first + k:first + k + tt, :]

    mu = jnp.mean(acc, axis=-1, keepdims=True)
    d = acc - mu
    var = jnp.mean(d * d, axis=-1, keepdims=True)
    y = d * lax.rsqrt(var + EPS) * lg_ref[...] + lb_ref[...]
    o_ref[...] = (y * jax.nn.sigmoid(y)).astype(BF16)

    @pl.when(t == pl.num_programs(1) - 1)
    def _():
        nb_ref[0] = uf_ref[n_last:n_last + CONV_HALO, :]

    ub_ref[0:CONV_HALO, :] = ub_ref[tt:tt + CONV_HALO, :]
    uf_ref[0:CONV_HALO, :] = uf_ref[tt:tt + CONV_HALO, :]


def _conv(z, buf, w, b, ln_g, ln_b, batch, t_pad, n_real):
    tt = min(t_pad, 512)
    nt = t_pad // tt
    n_last = n_real - (nt - 1) * tt
    first = CONV_HALO - (CONV_K - 1)
    buf_p = jnp.pad(buf, ((0, 0), (first, 0), (0, 0)))
    w_p = jnp.pad(w, ((0, CONV_HALO - CONV_K), (0, 0)))
    cw = CONV_WIDTH
    vec = lambda v: v.reshape(1, cw)
    o_c, nb = pl.pallas_call(
        functools.partial(_conv_body, tt=tt, n_last=n_last),
        grid=(batch, nt),
        in_specs=[
            pl.BlockSpec((tt, cw), lambda bb, t: (bb * nt + t, COL_CA // cw)),
            pl.BlockSpec((tt, cw), lambda bb, t: (bb * nt + t, COL_CG // cw)),
            pl.BlockSpec((1, CONV_HALO, cw), lambda bb, t: (bb, 0, 0)),
            pl.BlockSpec((CONV_HALO, cw), lambda bb, t: (0, 0)),
            pl.BlockSpec((1, cw), lambda bb, t: (0, 0)),
            pl.BlockSpec((1, cw), lambda bb, t: (0, 0)),
            pl.BlockSpec((1, cw), lambda bb, t: (0, 0)),
        ],
        out_specs=[
            pl.BlockSpec((tt, cw), lambda bb, t: (bb * nt + t, 0)),
            pl.BlockSpec((1, CONV_HALO, cw), lambda bb, t: (bb, 0, 0)),
        ],
        out_shape=[
            jax.ShapeDtypeStruct((batch * t_pad, cw), BF16),
            jax.ShapeDtypeStruct((batch, CONV_HALO, cw), F32),
        ],
        scratch_shapes=[pltpu.VMEM((CONV_HALO + tt, cw), F32), pltpu.VMEM((CONV_HALO + tt, cw), F32)],
        compiler_params=_params("arbitrary", "arbitrary"),
        name="conv",
    )(z, z, buf_p, w_p, vec(b), vec(ln_g), vec(ln_b))
    return o_c, nb[:, first:, :]


def _suffix_matrix(bk):
    j = np.arange(bk)[:, None]
    s = np.arange(bk)[None, :]
    m = np.concatenate([(j > s).astype(np.float32), np.ones((bk, LANES), np.float32)], axis=1)
    return jnp.asarray(m, dtype=BF16)


def _sb_tiles(qs, kblks, vblks, u, biases, carry, masked):
    n = len(qs)
    bq = qs[0].shape[0]
    bk = kblks[0].shape[0]
    zs, sps = [], []
    for j in range(n):
        s = lax.dot_general(qs[j], kblks[j], NT_DIMS, preferred_element_type=F32)
        z = s * (SB_DIM ** -0.5) + biases[j]
        zs.append(z)
        sps.append(jnp.maximum(z, 0.0) + jnp.log(1.0 + jnp.exp(-jnp.abs(z))))
    if masked:
        causal = lax.broadcasted_iota(I32, (bq, bk), 1) < lax.broadcasted_iota(I32, (bq, bk), 0)
    rts = []
    for j in range(n):
        spm = jnp.where(causal, sps[j], 0.0) if masked else sps[j]
        rts.append(jnp.dot(spm.astype(BF16), u, preferred_element_type=F32))
    out = []
    for j in range(n):
        c, acc = carry[2 * j], carry[2 * j + 1]
        c_all = c if bk == LANES else jnp.concatenate([c] * (bk // LANES), axis=1)
        a = jnp.exp((zs[j] - sps[j]) - (c_all + rts[j][:, :bk]))
        if masked:
            a = jnp.where(causal, a, 0.0)
        out += [c + rts[j][:, bk:], acc + jnp.dot(a.astype(BF16), vblks[j], preferred_element_type=F32)]
    return tuple(out)


def _sb_prompt_body(bias_ref, q_ref, k_ref, v_ref, u_ref, o_ref, kb_ref, vb_ref, *, bq, hp):
    hg = pl.program_id(1)
    i = pl.program_id(2)

    @pl.when(i == 0)
    def _():
        kb_ref[...] = k_ref[...].astype(BF16)
        vb_ref[...] = v_ref[...].astype(BF16)

    u = u_ref[...]
    cols = [slice(j * SB_DIM, (j + 1) * SB_DIM) for j in range(hp)]
    qs = [q_ref[:, cs].astype(BF16) for cs in cols]
    biases = [bias_ref[hg * hp + j] for j in range(hp)]

    def tiles(st, carry, masked):
        return _sb_tiles(qs, [kb_ref[pl.ds(st, bq), cs] for cs in cols], [vb_ref[pl.ds(st, bq), cs] for cs in cols],
                         u, biases, carry, masked)

    zero = jnp.zeros((bq, LANES), F32)
    carry = tiles(pl.multiple_of(i * bq, bq), (zero,) * (2 * hp), True)
    carry = lax.fori_loop(0, i, lambda n, cr: tiles(pl.multiple_of((i - 1 - n) * bq, bq), cr, False), carry)
    for j in range(hp):
        o_ref[:, cols[j]] = carry[2 * j + 1].astype(BF16)


def _sb_prompt(z, sb_bias, batch, t, bq=256, hp=4):
    nq = t // bq
    hw = hp * SB_DIM
    return pl.pallas_call(
        functools.partial(_sb_prompt_body, bq=bq, hp=hp),
        grid=(batch, SB_HEADS // hp, nq),
        in_specs=[
            pl.BlockSpec(memory_space=pltpu.SMEM),
            pl.BlockSpec((bq, hw), lambda b, h, i: (b * nq + i, COL_SQ // hw + h)),
            pl.BlockSpec((t, hw), lambda b, h, i: (b, COL_SK // hw + h)),
            pl.BlockSpec((t, hw), lambda b, h, i: (b, COL_SV // hw + h)),
            pl.BlockSpec((bq, bq + LANES), lambda b, h, i: (0, 0)),
        ],
        out_specs=pl.BlockSpec((bq, hw), lambda b, h, i: (b * nq + i, h)),
        out_shape=jax.ShapeDtypeStruct((batch * t, SB_WIDTH), BF16),
        scratch_shapes=[pltpu.VMEM((t, hw), BF16), pltpu.VMEM((t, hw), BF16)],
        compiler_params=_params("arbitrary", "arbitrary", "arbitrary"),
        name="sb_prompt",
    )(sb_bias, z, z, z, _suffix_matrix(bq))


SB_PAGES_PER_STEP = 8


def _sb_sample_body(pt_ref, bias_ref, q_ref, kn_ref, vn_ref, ck_hbm, cv_hbm, u_ref, o_ref,
                    kbuf, vbuf, sem, c_ref, acc_ref, *, layer, n_pages, pps):
    b = pl.program_id(0)
    s = pl.program_id(1)
    ns = pl.num_programs(1)
    step = b * ns + s
    slot = step % 2
    tp = SAMPLE_T_PAD

    def start_fetch(bb, ss, sl):
        for j in range(pps):
            page = pt_ref[bb, n_pages - 1 - (ss * pps + j)]
            for h in range(SB_HEADS):
                pltpu.make_async_copy(ck_hbm.at[layer, page, :, h, :], kbuf.at[sl, j, h], sem.at[0, sl]).start()
                pltpu.make_async_copy(cv_hbm.at[layer, page, :, h, :], vbuf.at[sl, j, h], sem.at[1, sl]).start()

    def wait_fetch(sl):
        pltpu.make_async_copy(kbuf.at[sl], kbuf.at[sl], sem.at[0, sl]).wait()
        pltpu.make_async_copy(vbuf.at[sl], vbuf.at[sl], sem.at[1, sl]).wait()

    @pl.when(step == 0)
    def _():
        start_fetch(0, 0, 0)

    @pl.when(step + 1 < pl.num_programs(0) * ns)
    def _():
        last = s == ns - 1
        start_fetch(jnp.where(last, b + 1, b), jnp.where(last, 0, s + 1), 1 - slot)

    u = u_ref[...]
    qs = [q_ref[:, h * SB_DIM:(h + 1) * SB_DIM].astype(BF16) for h in range(SB_HEADS)]
    bias = jnp.concatenate([jnp.full((tp, LANES), bias_ref[h], F32) for h in range(SB_HEADS)], axis=0)

    def attend(n_blocks, head_k, head_v, c, acc, masked):
        zs, sps, rts = [], [], []
        for j in range(n_blocks):
            sc = jnp.concatenate([lax.dot_general(qs[h], head_k(j, h).astype(BF16), NT_DIMS, preferred_element_type=F32)
                                  for h in range(SB_HEADS)], axis=0)
            z = sc * (SB_DIM ** -0.5) + bias
            zs.append(z)
            sps.append(jnp.maximum(z, 0.0) + jnp.log(1.0 + jnp.exp(-jnp.abs(z))))
        if masked:
            t = lax.broadcasted_iota(I32, zs[0].shape, 0) & (tp - 1)
            causal = lax.broadcasted_iota(I32, zs[0].shape, 1) < t
        for j in range(n_blocks):
            spm = jnp.where(causal, sps[j], 0.0) if masked else sps[j]
            rts.append(jnp.dot(spm.astype(BF16), u, preferred_element_type=F32))
        pvs = []
        for j in range(n_blocks):
            a = jnp.exp((zs[j] - sps[j]) - (c + rts[j][:, :PAGE_SIZE]))
            if masked:
                a = jnp.where(causal, a, 0.0)
            ab = a.astype(BF16)
            pvs.append(jnp.concatenate([jnp.dot(ab[h * tp:(h + 1) * tp], head_v(j, h).astype(BF16),
                                                preferred_element_type=F32) for h in range(SB_HEADS)], axis=0))
            c = c + rts[j][:, PAGE_SIZE:]
        return c, acc + functools.reduce(jnp.add, pvs)

    @pl.when(s == 0)
    def _():
        zero = jnp.zeros((SB_HEADS * tp, LANES), F32)
        c, acc = attend(1, lambda j, h: kn_ref[0, :, h * SB_DIM:(h + 1) * SB_DIM],
                        lambda j, h: vn_ref[0, :, h * SB_DIM:(h + 1) * SB_DIM], zero, zero, True)
        c_ref[...] = c
        acc_ref[...] = acc

    wait_fetch(slot)
    c, acc = attend(pps, lambda j, h: kbuf[slot, j, h], lambda j, h: vbuf[slot, j, h], c_ref[...], acc_ref[...], False)
    c_ref[...] = c
    acc_ref[...] = acc

    @pl.when(s == ns - 1)
    def _():
        for h in range(SB_HEADS):
            o_ref[:, h * SB_DIM:(h + 1) * SB_DIM] = acc[h * tp:(h + 1) * tp].astype(BF16)


def _sb_sample(z, cache_k, cache_v, layer, page_table, sb_bias, batch):
    n_pages = page_table.shape[1]
    pps = SB_PAGES_PER_STEP
    assert n_pages % pps == 0
    tp = SAMPLE_T_PAD
    w = SB_WIDTH

    def new_page(col):
        rows = z[:, col:col + w].reshape(batch, tp, w)
        return jnp.pad(rows, ((0, 0), (0, PAGE_SIZE - tp), (0, 0)))

    page_buf = pltpu.VMEM((2, pps, SB_HEADS, PAGE_SIZE, SB_DIM), F32)
    grid_spec = pltpu.PrefetchScalarGridSpec(
        num_scalar_prefetch=1,
        grid=(batch, n_pages // pps),
        in_specs=[
            pl.BlockSpec(memory_space=pltpu.SMEM),
            pl.BlockSpec((tp, w), lambda b, s, pt: (b, COL_SQ // w)),
            pl.BlockSpec((1, PAGE_SIZE, w), lambda b, s, pt: (b, 0, 0)),
            pl.BlockSpec((1, PAGE_SIZE, w), lambda b, s, pt: (b, 0, 0)),
            pl.BlockSpec(memory_space=pl.ANY),
            pl.BlockSpec(memory_space=pl.ANY),
            pl.BlockSpec((PAGE_SIZE, PAGE_SIZE + LANES), lambda b, s, pt: (0, 0)),
        ],
        out_specs=pl.BlockSpec((tp, w), lambda b, s, pt: (b, 0)),
        scratch_shapes=[page_buf, page_buf, pltpu.SemaphoreType.DMA((2, 2)),
                        pltpu.VMEM((SB_HEADS * tp, LANES), F32), pltpu.VMEM((SB_HEADS * tp, LANES), F32)],
    )
    return pl.pallas_call(
        functools.partial(_sb_sample_body, layer=layer, n_pages=n_pages, pps=pps),
        grid_spec=grid_spec,
        out_shape=jax.ShapeDtypeStruct((batch * tp, w), BF16),
        compiler_params=_params("arbitrary", "arbitrary"),
        name="sb_sample",
    )(page_table, sb_bias, z, new_page(COL_SK), new_page(COL_SV), cache_k, cache_v, _suffix_matrix(PAGE_SIZE))


ROUTE_ROWS = SUBLANES
ROUTER_PAD = LANES


def _route_rows(lt):
    lg = [lt[g:g + 1, :] for g in range(N_GROUPS)]
    m = functools.reduce(jnp.maximum, lg)
    gi = jnp.full(m.shape, N_GROUPS - 1, I32)
    for g in range(N_GROUPS - 2, -1, -1):
        gi = jnp.where(lg[g] == m, g, gi)
    den = functools.reduce(jnp.add, [jnp.exp(x - m) for x in lg])
    p_group = 1.0 / den

    def expert_logit(k):
        out = lt[N_GROUPS + (N_GROUPS - 1) * EXPERTS_PER_GROUP + k:N_GROUPS + (N_GROUPS - 1) * EXPERTS_PER_GROUP + k + 1, :]
        for g in range(N_GROUPS - 2, -1, -1):
            r = N_GROUPS + g * EXPERTS_PER_GROUP + k
            out = jnp.where(gi == g, lt[r:r + 1, :], out)
        return out

    le = [expert_logit(k) for k in range(EXPERTS_PER_GROUP)]
    v1 = functools.reduce(jnp.maximum, le)
    i1 = jnp.full(m.shape, EXPERTS_PER_GROUP - 1, I32)
    for k in range(EXPERTS_PER_GROUP - 2, -1, -1):
        i1 = jnp.where(le[k] == v1, k, i1)
    rest = [jnp.where(i1 == k, -jnp.inf, le[k]) for k in range(EXPERTS_PER_GROUP)]
    v2 = functools.reduce(jnp.maximum, rest)
    i2 = jnp.full(m.shape, EXPERTS_PER_GROUP - 1, I32)
    for k in range(EXPERTS_PER_GROUP - 2, -1, -1):
        i2 = jnp.where(rest[k] == v2, k, i2)
    e2 = jnp.exp(v2 - v1)
    inv = 1.0 / (1.0 + e2)
    rows = [p_group * inv, p_group * (e2 * inv),
            (gi * EXPERTS_PER_GROUP + i1).astype(F32), (gi * EXPERTS_PER_GROUP + i2).astype(F32)]
    rows += [jnp.zeros_like(m)] * (ROUTE_ROWS - len(rows))
    return jnp.concatenate(rows, axis=0)


def _out_proj_body(x_ref, or_ref, oc_ref, os_ref, w_ref, g_ref, wr_ref, br_ref, x2_ref, h2_ref, rt_ref):
    y = jnp.dot(or_ref[...], w_ref[0:RET_WIDTH, :], preferred_element_type=F32)
    y = y + jnp.dot(oc_ref[...], w_ref[RET_WIDTH:RET_WIDTH + CONV_WIDTH, :], preferred_element_type=F32)
    y = y + jnp.dot(os_ref[...], w_ref[RET_WIDTH + CONV_WIDTH:, :], preferred_element_type=F32)
    x2 = x_ref[...] + y
    x2_ref[...] = x2
    ms = jnp.mean(x2 * x2, axis=-1, keepdims=True)
    h = x2 * lax.rsqrt(ms + EPS) * g_ref[...]
    h2_ref[...] = h
    lt = lax.dot_general(wr_ref[...], h.astype(BF16), NT_DIMS, preferred_element_type=F32)
    rt_ref[...] = _route_rows(lt + br_ref[...])


def _out_proj(x, o_r, o_c, o_s, w_out_bf16, norm_ffn, rg_w, rg_b, re_w, re_b, tm):
    n, d = x.shape
    wr = jnp.concatenate([rg_w, re_w], axis=1).T
    wr = jnp.pad(wr, ((0, ROUTER_PAD - wr.shape[0]), (0, 0))).astype(BF16)
    br = jnp.pad(jnp.concatenate([rg_b, re_b]), (0, ROUTER_PAD - N_GROUPS - N_EXPERTS)).reshape(ROUTER_PAD, 1)
    row = lambda width: pl.BlockSpec((tm, width), lambda i: (i, 0))
    full = lambda a, b: pl.BlockSpec((a, b), lambda i: (0, 0))
    return pl.pallas_call(
        _out_proj_body,
        grid=(n // tm,),
        in_specs=[row(d), row(RET_WIDTH), row(CONV_WIDTH), row(SB_WIDTH), full(d, d), full(1, d),
                  full(ROUTER_PAD, d), full(ROUTER_PAD, 1)],
        out_specs=[row(d), row(d), pl.BlockSpec((ROUTE_ROWS, tm), lambda i: (0, i))],
        out_shape=[jax.ShapeDtypeStruct((n, d), F32), jax.ShapeDtypeStruct((n, d), F32),
                   jax.ShapeDtypeStruct((ROUTE_ROWS, n), F32)],
        compiler_params=_params("arbitrary"),
        name="out_proj",
    )(x, o_r, o_c, o_s, w_out_bf16, norm_ffn.reshape(1, d), wr, br)


def _moe_plan(route, n_tok, bm):
    experts = route[TOP_K:2 * TOP_K, :].astype(I32)
    a = TOP_K * n_tok
    n_blocks = (a + N_EXPERTS * (bm - 1) + bm - 1) // bm
    r = n_blocks * bm
    flat_e = experts.reshape(-1)
    onehot = (flat_e[:, None] == jnp.arange(N_EXPERTS, dtype=I32)[None, :]).astype(I32)
    rank = jnp.sum((jnp.cumsum(onehot, axis=0) - onehot) * onehot, axis=1)
    counts = jnp.sum(onehot, axis=0)
    padded = (counts + bm - 1) // bm * bm
    pad_ends = jnp.cumsum(padded)
    pad_starts = pad_ends - padded
    dest = pad_starts[flat_e] + rank
    slot_of_row = jnp.full((r,), -1, I32).at[dest].set(jnp.arange(a, dtype=I32))
    row_dst = jnp.maximum(slot_of_row, 0)
    row_tok = jnp.where(row_dst >= n_tok, row_dst - n_tok, row_dst)
    n_used = (pad_ends[-1] // bm).astype(I32)
    blk = jnp.minimum(jnp.arange(n_blocks, dtype=I32), n_used - 1) * bm
    blk_expert = jnp.minimum(jnp.searchsorted(pad_ends, blk, side='right'), N_EXPERTS - 1).astype(I32)
    blk_valid = jnp.clip(pad_starts[blk_expert] + counts[blk_expert] - blk, 0, bm).astype(I32)
    return row_tok, row_dst, blk_expert, blk_valid, n_used.reshape(1), n_blocks


def _moe_body(tok_ref, dst_ref, be_ref, bv_ref, nu_ref, h_hbm, wg_ref, wu_ref, wd_ref, y_hbm,
              xbuf, ybuf, wgb, wub, wdb, gsem, ssem, *, bm):
    i = pl.program_id(0)
    n_used = nu_ref[0]
    slot = i % 2

    def for_real_rows(blk, row):
        nv = bv_ref[blk]

        def group(g, carry):
            for r in range(SUBLANES):
                row(g, r)
            return carry

        def single(j, carry):
            row(j // SUBLANES, j % SUBLANES)
            return carry

        full = nv // SUBLANES
        lax.fori_loop(0, full, group, 0)
        lax.fori_loop(full * SUBLANES, nv, single, 0)

    def wait_real_rows(blk, buf, sem):
        nv = bv_ref[blk]
        p = 1
        while p <= bm:
            chunk = buf.at[0, pl.ds(0, p), :] if p < SUBLANES else buf.at[pl.ds(0, p // SUBLANES)]

            @pl.when((nv & p) != 0)
            def _(chunk=chunk):
                pltpu.make_async_copy(chunk, chunk, sem).wait()
            p *= 2

    def start_gather(blk, s):
        def row(g, r):
            tok = tok_ref[blk * bm + g * SUBLANES + r]
            pltpu.make_async_copy(h_hbm.at[pl.ds(tok, 1), :], xbuf.at[s, g, pl.ds(r, 1), :], gsem.at[s]).start()
        for_real_rows(blk, row)

    def wait_gather(blk, s):
        wait_real_rows(blk, xbuf.at[s], gsem.at[s])

    def start_scatter(blk, s):
        def row(g, r):
            dst = dst_ref[blk * bm + g * SUBLANES + r]
            pltpu.make_async_copy(ybuf.at[s, g, pl.ds(r, 1), :], y_hbm.at[pl.ds(dst, 1), :], ssem.at[s]).start()
        for_real_rows(blk, row)

    def wait_scatter(blk, s):
        wait_real_rows(blk, ybuf.at[s], ssem.at[s])

    @pl.when(i == 0)
    def _():
        xbuf[...] = jnp.zeros(xbuf.shape, F32)
        start_gather(0, 0)

    @pl.when(i + 1 < n_used)
    def _():
        start_gather(i + 1, 1 - slot)

    @pl.when(i < n_used)
    def _():
        wait_gather(i, slot)

        @pl.when(i >= 2)
        def _():
            wait_scatter(i - 2, slot)

        @pl.when((i == 0) | (be_ref[i] != be_ref[jnp.maximum(i - 1, 0)]))
        def _():
            wgb[...] = wg_ref[0, 0].astype(BF16)
            wub[...] = wu_ref[0, 0].astype(BF16)
            wdb[...] = wd_ref[0, 0].astype(BF16)

        x = xbuf[slot].reshape(bm, D_MODEL).astype(BF16)
        g = jnp.dot(x, wgb[...], preferred_element_type=F32)
        u = jnp.dot(x, wub[...], preferred_element_type=F32)
        hm = (g * jax.nn.sigmoid(g) * u).astype(BF16)
        y = jnp.dot(hm, wdb[...], preferred_element_type=F32)
        ybuf[slot] = y.astype(BF16).astype(F32).reshape(bm // SUBLANES, SUBLANES, D_MODEL)
        start_scatter(i, slot)

    @pl.when(i == pl.num_programs(0) - 1)
    def _():
        wait_scatter(n_used - 1, (n_used - 1) % 2)

        @pl.when(n_used >= 2)
        def _():
            wait_scatter(n_used - 2, n_used % 2)


def _moe(h2, route, w_gate, w_up, w_down, layer, bm):
    n_tok, d = h2.shape
    row_tok, row_dst, blk_expert, blk_valid, n_used, n_blocks = _moe_plan(route, n_tok, bm)
    wspec = lambda a, b: pl.BlockSpec((1, 1, a, b), lambda i, tok, dst, be, bv, nu: (layer, be[i], 0, 0))
    grid_spec = pltpu.PrefetchScalarGridSpec(
        num_scalar_prefetch=5,
        grid=(n_blocks,),
        in_specs=[
            pl.BlockSpec(memory_space=pl.ANY),
            wspec(d, D_EXPERT), wspec(d, D_EXPERT), wspec(D_EXPERT, d),
        ],
        out_specs=pl.BlockSpec(memory_space=pl.ANY),
        scratch_shapes=[pltpu.VMEM((2, bm // SUBLANES, SUBLANES, d), F32), pltpu.VMEM((2, bm // SUBLANES, SUBLANES, d), F32),
                        pltpu.VMEM((d, D_EXPERT), BF16), pltpu.VMEM((d, D_EXPERT), BF16), pltpu.VMEM((D_EXPERT, d), BF16),
                        pltpu.SemaphoreType.DMA((2,)), pltpu.SemaphoreType.DMA((2,))],
    )
    return pl.pallas_call(
        functools.partial(_moe_body, bm=bm),
        grid_spec=grid_spec,
        out_shape=jax.ShapeDtypeStruct((TOP_K * n_tok, d), F32),
        compiler_params=_params("arbitrary"),
        name="moe",
    )(row_tok, row_dst, blk_expert, blk_valid, n_used, h2, w_gate, w_up, w_down)


def _combine_body(x_ref, y0_ref, y1_ref, rt_ref, g_ref, o_ref, *, final_norm):
    gates = rt_ref[...].astype(BF16).astype(F32).T
    x = x_ref[...] + (y0_ref[...] * gates[:, 0:1] + y1_ref[...] * gates[:, 1:2])
    if final_norm:
        ms = jnp.mean(x * x, axis=-1, keepdims=True)
        x = x * lax.rsqrt(ms + EPS) * g_ref[...]
    o_ref[...] = x


def _combine(x2, y, route, norm_gain, final_norm, tm):
    n, d = x2.shape
    nb = n // tm
    return pl.pallas_call(
        functools.partial(_combine_body, final_norm=final_norm),
        grid=(nb,),
        in_specs=[
            pl.BlockSpec((tm, d), lambda i: (i, 0)),
            pl.BlockSpec((tm, d), lambda i: (i, 0)),
            pl.BlockSpec((tm, d), lambda i: (i + nb, 0)),
            pl.BlockSpec((ROUTE_ROWS, tm), lambda i: (0, i)),
            pl.BlockSpec((1, d), lambda i: (0, 0)),
        ],
        out_specs=pl.BlockSpec((tm, d), lambda i: (i, 0)),
        out_shape=jax.ShapeDtypeStruct((n, d), F32),
        compiler_params=_params("arbitrary"),
        name="combine",
    )(x2, y, y, route, norm_gain.reshape(1, d))


def _layer(x, l, last, group, w, state):
    batch, t_pad, n_real = group['batch'], group['t_pad'], group['n_real']
    if group['paged']:
        z = _norm_proj(x, w['norm_mix'][l], w['w_in'], l, group['tm'])
        k_new = z[:, COL_SK:COL_SK + SB_WIDTH].reshape(batch, t_pad, SB_HEADS, SB_DIM)[:, :n_real]
        v_new = z[:, COL_SV:COL_SV + SB_WIDTH].reshape(batch, t_pad, SB_HEADS, SB_DIM)[:, :n_real]
    else:
        z, k_new, v_new = _norm_proj(x, w['norm_mix'][l], w['w_in'], l, group['tm'], seq_len=t_pad)
    o_r, s_ret = _retention(z, state['ret'], w['ret_gn_gain'][l], w['ret_gn_bias'][l], batch, t_pad, n_real, group['q_start'])
    o_c, new_buf = _conv(z, state['conv'], w['conv_w'][l], w['conv_b'][l], w['conv_ln_gain'][l], w['conv_ln_bias'][l],
                         batch, t_pad, n_real)
    if group['paged']:
        o_s = _sb_sample(z, state['cache_k'], state['cache_v'], l, state['page_table'], w['sb_bias'][l], batch)
    else:
        o_s = _sb_prompt(z, w['sb_bias'][l], batch, t_pad)
    x2, h2, route = _out_proj(x, o_r, o_c, o_s, w['w_out'][l], w['norm_ffn'][l], w['router_group_w'][l],
                              w['router_group_b'][l], w['router_expert_w'][l], w['router_expert_b'][l], group['tm_out'])
    y = _moe(h2, route, w['w_gate'], w['w_up'], w['w_down'], l, group['bm'])
    x3 = _combine(x2, y, route, w['norm_final'], last, group['tm_out'])
    return x3, s_ret, new_buf, k_new, v_new


def kernel(x_prompt, x_sample, state_ret, state_conv, cache_k, cache_v, page_table, norm_mix, w_in, ret_gn_gain, ret_gn_bias, conv_w, conv_b, conv_ln_gain, conv_ln_bias, sb_bias, w_out, norm_ffn, router_group_w, router_group_b, router_expert_w, router_expert_b, w_gate, w_up, w_down, norm_final):
    b, t, d = x_prompt.shape
    db, dt, _ = x_sample.shape
    depth = w_in.shape[0]
    n_past = page_table.shape[1] * PAGE_SIZE
    w = dict(norm_mix=norm_mix, w_in=w_in.astype(BF16), ret_gn_gain=ret_gn_gain, ret_gn_bias=ret_gn_bias,
             conv_w=conv_w, conv_b=conv_b, conv_ln_gain=conv_ln_gain, conv_ln_bias=conv_ln_bias, sb_bias=sb_bias,
             w_out=w_out.astype(BF16), norm_ffn=norm_ffn, router_group_w=router_group_w, router_group_b=router_group_b,
             router_expert_w=router_expert_w, router_expert_b=router_expert_b,
             w_gate=w_gate, w_up=w_up, w_down=w_down, norm_final=norm_final)
    prompt = dict(batch=b, t_pad=t, n_real=t, q_start=0, tm=1024, tm_out=256, bm=256, paged=False)
    sample = dict(batch=db, t_pad=SAMPLE_T_PAD, n_real=dt, q_start=n_past, tm=db * SAMPLE_T_PAD,
                  tm_out=db * SAMPLE_T_PAD, bm=SUBLANES, paged=True)

    xp = x_prompt.reshape(b * t, d)
    xs = jnp.pad(x_sample, ((0, 0), (0, SAMPLE_T_PAD - dt), (0, 0))).reshape(db * SAMPLE_T_PAD, d)
    outs_p, outs_s = [], []
    for l in range(depth):
        last = l == depth - 1
        st_p = dict(ret=jnp.zeros((b, RET_HEADS, RET_DIM, RET_DIM), F32), conv=jnp.zeros((b, CONV_K - 1, CONV_WIDTH), F32))
        xp, *lp = _layer(xp, l, last, prompt, w, st_p)
        outs_p.append(lp)
        st_s = dict(ret=state_ret[l], conv=state_conv[l], cache_k=cache_k, cache_v=cache_v, page_table=page_table)
        xs, *ls = _layer(xs, l, last, sample, w, st_s)
        outs_s.append(ls)
    y_prompt = xp.reshape(b, t, d)
    y_sample = xs.reshape(db, SAMPLE_T_PAD, d)[:, :dt]
    stack = lambda outs, j: jnp.stack([o[j] for o in outs])
    return (y_prompt, y_sample,
            stack(outs_p, 0), stack(outs_p, 1), stack(outs_p, 2), stack(outs_p, 3),
            stack(outs_s, 0), stack(outs_s, 1), stack(outs_s, 2), stack(outs_s, 3))
```
